```python
import math
import jax, jax.numpy as jnp
from jax import lax
import numpy as np

D_MODEL = 4096
BATCH = 2
SEQ = 8192
DEPTH = 2

SSD_EXPAND = 2
D_INNER = SSD_EXPAND * D_MODEL
SSD_HEADDIM = 64
SSD_HEADS = D_INNER // SSD_HEADDIM
SSD_GROUPS = 8
SSD_STATE = 128
SSD_CONV = 4
SSD_CHUNK = 128
SSD_CONV_DIM = D_INNER + 2 * SSD_GROUPS * SSD_STATE
MLSTM_HEADS = 8
MLSTM_QK = D_MODEL // 2
MLSTM_V = D_MODEL
MLSTM_DK = MLSTM_QK // MLSTM_HEADS
MLSTM_DV = MLSTM_V // MLSTM_HEADS
MLSTM_CHUNK = 64
PEER_HEADS = 8
PEER_NKEYS = 128
PEER_EXPERTS = PEER_NKEYS * PEER_NKEYS
PEER_DKEY = 256
PEER_HALF = PEER_DKEY // 2
PEER_TOPK = 16
PEER_BLOCK = 64
DN_ALPHA = (2 * DEPTH) ** 0.25
DN_BETA = (8 * DEPTH) ** -0.25
LN_EPS = 1e-5
RMS_EPS = 1e-6
IN_SPLITS = (D_INNER, SSD_CONV_DIM, SSD_HEADS,
             MLSTM_QK, MLSTM_QK, MLSTM_V, MLSTM_V, MLSTM_HEADS, MLSTM_HEADS,
             D_MODEL, D_MODEL)
D_IN_PROJ = sum(IN_SPLITS)

kernel_name = 'hybrid_ssd_mlstm_peer_deepnorm'


def layer_norm(x, g, b):
    xf = x.astype(jnp.float32)
    mu = jnp.mean(xf, axis=-1, keepdims=True)
    var = jnp.mean(jnp.square(xf - mu), axis=-1, keepdims=True)
    y = (xf - mu) * lax.rsqrt(var + LN_EPS)
    return (y * g.astype(jnp.float32) + b.astype(jnp.float32)).astype(x.dtype)


def causal_depthwise_conv(u, w, b):
    width = w.shape[0]
    up = jnp.pad(u, ((0, 0), (width - 1, 0), (0, 0)))
    out = lax.conv_general_dilated(up, w[:, None, :], window_strides=(1,), padding='VALID',
                                   dimension_numbers=('NWC', 'WIO', 'NWC'),
                                   feature_group_count=u.shape[-1])
    return out + b


def ssd_chunked_scan(xs, dt, a, b_mat, c_mat):
    bsz, t_len, n_heads, p_dim = xs.shape
    n_groups, n_state = b_mat.shape[2:]
    j_dim = n_heads // n_groups
    blk = SSD_CHUNK
    n_chunks = t_len // blk

    def to_chunks(t, tail):
        return jnp.moveaxis(t.reshape((bsz, n_chunks, blk) + tail), 1, 0)

    xdt = to_chunks(xs * dt[..., None], (n_groups, j_dim, p_dim))
    la = to_chunks(dt * a, (n_groups, j_dim))
    bc = to_chunks(b_mat, (n_groups, n_state))
    cc = to_chunks(c_mat, (n_groups, n_state))
    causal = jnp.tril(jnp.ones((blk, blk), dtype=bool))[None, :, :, None, None]

    def step(state, inp):
        xdt_c, la_c, b_c, c_c = inp
        acum = jnp.cumsum(la_c, axis=1)
        seg = acum[:, :, None] - acum[:, None, :]
        decay = jnp.exp(jnp.where(causal, seg, -jnp.inf))
        scores = jnp.einsum('btgn,bsgn->btsg', c_c, b_c)[..., None] * decay
        y = jnp.einsum('btsgj,bsgjp->btgjp', scores, xdt_c)
        y = y + jnp.einsum('btgn,bgjpn->btgjp', c_c, state) * jnp.exp(acum)[..., None]
        w_end = jnp.exp(acum[:, -1:] - acum)[..., None] * xdt_c
        state = (state * jnp.exp(acum[:, -1])[..., None, None]
                 + jnp.einsum('bsgjp,bsgn->bgjpn', w_end, b_c))
        return state, y

    state0 = jnp.zeros((bsz, n_groups, j_dim, p_dim, n_state), xs.dtype)
    _, ys = lax.scan(step, state0, (xdt, la, bc, cc))
    return jnp.moveaxis(ys, 0, 1).reshape(bsz, t_len, n_heads, p_dim)


def ssd_branch(z, xbc, dt_pre, conv_w, conv_b, dt_bias, a_log, d_skip, norm_w):
    f32 = jnp.float32
    out_dtype = z.dtype
    bsz, t_len, _ = z.shape
    xbc = jax.nn.silu(causal_depthwise_conv(xbc, conv_w, conv_b)).astype(f32)
    gn = SSD_GROUPS * SSD_STATE
    xs, b_mat, c_mat = jnp.split(xbc, [D_INNER, D_INNER + gn], axis=-1)
    xs = xs.reshape(bsz, t_len, SSD_HEADS, SSD_HEADDIM)
    b_mat = b_mat.reshape(bsz, t_len, SSD_GROUPS, SSD_STATE)
    c_mat = c_mat.reshape(bsz, t_len, SSD_GROUPS, SSD_STATE)
    dt = jax.nn.softplus(dt_pre.astype(f32) + dt_bias.astype(f32))
    a = -jnp.exp(a_log.astype(f32))
    y = ssd_chunked_scan(xs, dt, a, b_mat, c_mat) + xs * d_skip.astype(f32)[:, None]
    y = y.reshape(bsz, t_len, D_INNER) * jax.nn.silu(z.astype(f32))
    yg = y.reshape(bsz, t_len, SSD_GROUPS, D_INNER // SSD_GROUPS)
    yg = yg * lax.rsqrt(jnp.mean(jnp.square(yg), axis=-1, keepdims=True) + RMS_EPS)
    return (yg.reshape(bsz, t_len, D_INNER) * norm_w.astype(f32)).astype(out_dtype)


def mlstm_chunked(q, k, v, logi, logf):
    bsz, t_len, n_heads, dk = q.shape
    dv = v.shape[-1]
    blk = MLSTM_CHUNK
    n_chunks = t_len // blk

    def to_chunks(t):
        return jnp.moveaxis(t.reshape((bsz, n_chunks, blk) + t.shape[2:]), 1, 0)

    causal = jnp.tril(jnp.ones((blk, blk), dtype=bool))[None, :, :, None]

    def step(carry, inp):
        c_st, n_st, m_st = carry
        q_c, k_c, v_c, li, lf = inp
        bcum = jnp.cumsum(lf, axis=1)
        dlog = bcum[:, :, None] - bcum[:, None, :] + li[:, None, :]
        dlog = jnp.where(causal, dlog, -jnp.inf)
        inter_log = bcum + m_st[:, None]
        m_t = jnp.maximum(inter_log, jnp.max(dlog, axis=2))
        w_intra = jnp.exp(dlog - m_t[:, :, None])
        w_inter = jnp.exp(inter_log - m_t)
        s = jnp.einsum('bthk,bshk->btsh', q_c, k_c) * w_intra
        num = (jnp.einsum('btsh,bshv->bthv', s, v_c)
               + w_inter[..., None] * jnp.einsum('bthk,bhkv->bthv', q_c, c_st))
        den = jnp.sum(s, axis=2) + w_inter * jnp.einsum('bthk,bhk->bth', q_c, n_st)
        h = num / jnp.maximum(jnp.abs(den), jnp.exp(-m_t))[..., None]
        end_log = bcum[:, -1:] - bcum + li
        m_new = jnp.maximum(bcum[:, -1] + m_st, jnp.max(end_log, axis=1))
        w_end = jnp.exp(end_log - m_new[:, None])
        w_carry = jnp.exp(bcum[:, -1] + m_st - m_new)
        kw = w_end[..., None] * k_c
        c_new = w_carry[..., None, None] * c_st + jnp.einsum('bshk,bshv->bhkv', kw, v_c)
        n_new = w_carry[..., None] * n_st + jnp.sum(kw, axis=1)
        return (c_new, n_new, m_new), h

    carry0 = (jnp.zeros((bsz, n_heads, dk, dv), q.dtype),
              jnp.zeros((bsz, n_heads, dk), q.dtype),
              jnp.zeros((bsz, n_heads), q.dtype))
    _, hs = lax.scan(step, carry0, (to_chunks(q), to_chunks(k), to_chunks(v),
                                    to_chunks(logi), to_chunks(logf)))
    return jnp.moveaxis(hs, 0, 1).reshape(bsz, t_len, n_heads, dv)


def mlstm_branch(q, k, v, o_pre, i_pre, f_pre, igate_b, fgate_b, norm_w):
    f32 = jnp.float32
    out_dtype = q.dtype
    bsz, t_len, _ = q.shape
    qh = q.astype(f32).reshape(bsz, t_len, MLSTM_HEADS, MLSTM_DK) * (MLSTM_DK ** -0.5)
    kh = k.astype(f32).reshape(bsz, t_len, MLSTM_HEADS, MLSTM_DK)
    vh = v.astype(f32).reshape(bsz, t_len, MLSTM_HEADS, MLSTM_DV)
    logi = i_pre.astype(f32) + igate_b.astype(f32)
    logf = jax.nn.log_sigmoid(f_pre.astype(f32) + fgate_b.astype(f32))
    h = mlstm_chunked(qh, kh, vh, logi, logf)
    h = h * lax.rsqrt(jnp.mean(jnp.square(h), axis=-1, keepdims=True) + RMS_EPS)
    h = h.reshape(bsz, t_len, MLSTM_V) * norm_w.astype(f32) * jax.nn.sigmoid(o_pre.astype(f32))
    return h.astype(out_dtype)


def hybrid_mixer(x, w_in, conv_w, conv_b, dt_bias, a_log, d_skip, ssm_norm_w, w_branch_a,
                 igate_b, fgate_b, mlstm_norm_w, w_branch_b, gate_b, w_out):
    proj = x @ w_in
    cuts = [int(c) for c in np.cumsum(IN_SPLITS)[:-1]]
    (z, xbc, dt_pre, q, k, v, o_pre, i_pre, f_pre, g_a, g_b) = jnp.split(proj, cuts, axis=-1)
    y_a = ssd_branch(z, xbc, dt_pre, conv_w, conv_b, dt_bias, a_log, d_skip, ssm_norm_w) @ w_branch_a
    y_b = mlstm_branch(q, k, v, o_pre, i_pre, f_pre, igate_b, fgate_b, mlstm_norm_w) @ w_branch_b
    gate_a = jax.nn.sigmoid(g_a + gate_b[:D_MODEL])
    gate_m = jax.nn.sigmoid(g_b + gate_b[D_MODEL:])
    return (gate_a * y_a + gate_m * y_b) @ w_out


def peer_ffn(x, wq, k1, k2, u_tab, v_tab):
    bsz, t_len, d = x.shape
    q = (x @ wq).reshape(bsz, t_len, PEER_HEADS, 2, PEER_HALF)
    s1 = jnp.einsum('bthd,hnd->bthn', q[..., 0, :], k1)
    s2 = jnp.einsum('bthd,hnd->bthn', q[..., 1, :], k2)
    v1, i1 = lax.top_k(s1, PEER_TOPK)
    v2, i2 = lax.top_k(s2, PEER_TOPK)
    cand = (v1[..., :, None] + v2[..., None, :]).reshape(bsz, t_len, PEER_HEADS, PEER_TOPK * PEER_TOPK)
    cidx = (i1[..., :, None] * PEER_NKEYS + i2[..., None, :]).reshape(cand.shape)
    top, pos = lax.top_k(cand, PEER_TOPK)
    eidx = jnp.take_along_axis(cidx, pos, axis=-1)
    gates = jax.nn.softmax(top.astype(jnp.float32), axis=-1).astype(x.dtype)
    n_blk = (bsz * t_len) // PEER_BLOCK
    sel = PEER_HEADS * PEER_TOPK
    xb = x.reshape(n_blk, PEER_BLOCK, d)
    eb = eidx.reshape(n_blk, PEER_BLOCK, sel)
    gb = gates.reshape(n_blk, PEER_BLOCK, sel)

    def block(args):
        x_blk, e_blk, g_blk = args
        act = jax.nn.gelu(jnp.einsum('td,ted->te', x_blk, u_tab[e_blk]), approximate=False)
        return jnp.einsum('te,ted->td', g_blk * act, v_tab[e_blk])

    y = lax.map(block, (xb, eb, gb))
    return y.reshape(bsz, t_len, d)


def setup_inputs(seed: int = 0) -> dict:
    key = jax.random.key(seed)
    ks = jax.random.split(key, 26)
    f32 = jnp.float32
    L = DEPTH

    def nrm(k, shape, scale):
        return jax.random.normal(k, shape, f32) * scale

    dt0 = jnp.exp(jax.random.uniform(ks[6], (L, SSD_HEADS), f32, math.log(1e-3), math.log(1e-1)))
    return {
        'x': nrm(ks[0], (BATCH, SEQ, D_MODEL), 1.0),
        'emb_ln_g': 1.0 + nrm(ks[1], (D_MODEL,), 0.02),
        'emb_ln_b': nrm(ks[2], (D_MODEL,), 0.02),
        'w_in': nrm(ks[3], (L, D_MODEL, D_IN_PROJ), D_MODEL ** -0.5),
        'conv_w': nrm(ks[4], (L, SSD_CONV, SSD_CONV_DIM), SSD_CONV ** -0.5),
        'conv_b': nrm(ks[5], (L, SSD_CONV_DIM), 0.02),
        'dt_bias': dt0 + jnp.log(-jnp.expm1(-dt0)),
        'a_log': jnp.log(jax.random.uniform(ks[7], (L, SSD_HEADS), f32, 1.0, 16.0)),
        'd_skip': 1.0 + nrm(ks[8], (L, SSD_HEADS), 0.02),
        'ssm_norm_w': 1.0 + nrm(ks[9], (L, D_INNER), 0.02),
        'w_branch_a': nrm(ks[10], (L, D_INNER, D_MODEL), DN_BETA * D_INNER ** -0.5),
        'igate_b': nrm(ks[11], (L, MLSTM_HEADS), 0.1),
        'fgate_b': 3.0 + 3.0 * jax.random.uniform(ks[12], (L, MLSTM_HEADS), f32),
        'mlstm_norm_w': 1.0 + nrm(ks[13], (L, MLSTM_V), 0.02),
        'w_branch_b': nrm(ks[14], (L, MLSTM_V, D_MODEL), DN_BETA * MLSTM_V ** -0.5),
        'gate_b': nrm(ks[15], (L, 2 * D_MODEL), 0.02),
        'w_out': nrm(ks[16], (L, D_MODEL, D_MODEL), DN_BETA * D_MODEL ** -0.5),
        'ln1_g': 1.0 + nrm(ks[17], (L, D_MODEL), 0.02),
        'ln1_b': nrm(ks[18], (L, D_MODEL), 0.02),
        'peer_wq': nrm(ks[19], (L, D_MODEL, PEER_HEADS * PEER_DKEY), D_MODEL ** -0.5),
        'peer_k1': nrm(ks[20], (L, PEER_HEADS, PEER_NKEYS, PEER_HALF), PEER_HALF ** -0.5),
        'peer_k2': nrm(ks[21], (L, PEER_HEADS, PEER_NKEYS, PEER_HALF), PEER_HALF ** -0.5),
        'peer_u': nrm(ks[22], (L, PEER_EXPERTS, D_MODEL), D_MODEL ** -0.5),
        'peer_v': nrm(ks[23], (L, PEER_EXPERTS, D_MODEL), DN_BETA),
        'ln2_g': 1.0 + nrm(ks[24], (L, D_MODEL), 0.02),
        'ln2_b': nrm(ks[25], (L, D_MODEL), 0.02),
    }


def reference(x, emb_ln_g, emb_ln_b, w_in, conv_w, conv_b, dt_bias, a_log, d_skip, ssm_norm_w,
              w_branch_a, igate_b, fgate_b, mlstm_norm_w, w_branch_b, gate_b, w_out, ln1_g, ln1_b,
              peer_wq, peer_k1, peer_k2, peer_u, peer_v, ln2_g, ln2_b):
    h = layer_norm(x, emb_ln_g, emb_ln_b)
    for l in range(DEPTH):
        mix = hybrid_mixer(h, w_in[l], conv_w[l], conv_b[l], dt_bias[l], a_log[l], d_skip[l],
                           ssm_norm_w[l], w_branch_a[l], igate_b[l], fgate_b[l], mlstm_norm_w[l],
                           w_branch_b[l], gate_b[l], w_out[l])
        h = layer_norm(DN_ALPHA * h + mix, ln1_g[l], ln1_b[l])
        ffn = peer_ffn(h, peer_wq[l], peer_k1[l], peer_k2[l], peer_u[l], peer_v[l])
        h = layer_norm(DN_ALPHA * h + ffn, ln2_g[l], ln2_b[l])
    return h
```

```python
import functools
import math
from typing import NamedTuple

import jax
import jax.numpy as jnp
from jax import lax
from jax.experimental import pallas as pl
from jax.experimental.pallas import tpu as pltpu

F32 = jnp.float32
BF16 = jnp.bfloat16

V7X_LANES = 128
V7X_SUBLANES = 8
V7X_VMEM_BYTES = 64 * 1024 * 1024
VMEM_BUDGET_BYTES = 56 * 1024 * 1024

LN_EPS = 1e-5
RMS_EPS = 1e-6


class Cfg(NamedTuple):
    depth: int = 2
    d_model: int = 4096
    d_inner: int = 8192
    ssd_headdim: int = 64
    ssd_groups: int = 8
    ssd_state: int = 128
    ssd_conv: int = 4
    ssd_chunk: int = 128
    ml_heads: int = 8
    ml_qk: int = 2048
    ml_v: int = 4096
    ml_chunk: int = 256
    peer_heads: int = 8
    peer_nkeys: int = 128
    peer_dkey: int = 256
    peer_topk: int = 16

    @property
    def ssd_heads(self):
        return self.d_inner // self.ssd_headdim

    @property
    def group_heads(self):
        return self.ssd_heads // self.ssd_groups

    @property
    def group_width(self):
        return self.d_inner // self.ssd_groups

    @property
    def conv_dim(self):
        return self.d_inner + 2 * self.ssd_groups * self.ssd_state

    @property
    def ml_dk(self):
        return self.ml_qk // self.ml_heads

    @property
    def ml_dv(self):
        return self.ml_v // self.ml_heads

    @property
    def peer_half(self):
        return self.peer_dkey // 2

    @property
    def peer_experts(self):
        return self.peer_nkeys * self.peer_nkeys


def _params(semantics, vmem_bytes):
    return pltpu.CompilerParams(dimension_semantics=semantics,
                                vmem_limit_bytes=int(min(max(vmem_bytes, 16 * 1024 * 1024), VMEM_BUDGET_BYTES)))


def _nbytes(shape, dtype):
    return math.prod(shape) * jnp.dtype(dtype).itemsize


def _pick(n, prefs):
    for p in prefs:
        if n % p == 0:
            return p
    return n


def _ln_rows(x, g, b):
    mu = jnp.mean(x, axis=-1, keepdims=True)
    xc = x - mu
    var = jnp.mean(xc * xc, axis=-1, keepdims=True)
    return xc * lax.rsqrt(var + LN_EPS) * g + b


def _ln_kernel(x_ref, g_ref, b_ref, of_ref, ob_ref):
    y = _ln_rows(x_ref[...], g_ref[...], b_ref[...])
    of_ref[...] = y
    ob_ref[...] = y.astype(BF16)


def _res_ln_kernel(h_ref, f_ref, g_ref, b_ref, of_ref, ob_ref, *, alpha):
    y = _ln_rows(alpha * h_ref[...] + f_ref[...], g_ref[...], b_ref[...])
    of_ref[...] = y
    ob_ref[...] = y.astype(BF16)


def _res_ln_t_kernel(h_ref, ft_ref, g_ref, b_ref, of_ref, ob_ref, *, alpha):
    y = _ln_rows(alpha * h_ref[...] + ft_ref[...].T, g_ref[...], b_ref[...])
    of_ref[...] = y
    ob_ref[...] = y.astype(BF16)


def _ln_call(kernel_fn, row_inputs, g, b, n, d, tm, transposed_last=False):
    row_spec = pl.BlockSpec((tm, d), lambda i: (i, 0))
    vec_spec = pl.BlockSpec((1, d), lambda i: (0, 0))
    in_specs = []
    for k, _ in enumerate(row_inputs):
        if transposed_last and k == len(row_inputs) - 1:
            in_specs.append(pl.BlockSpec((d, tm), lambda i: (0, i)))
        else:
            in_specs.append(row_spec)
    vmem = 2 * (len(row_inputs) + 1) * _nbytes((tm, d), F32) + 2 * _nbytes((tm, d), BF16) + 4 * _nbytes((tm, d), F32)
    return pl.pallas_call(
        kernel_fn,
        grid=(n // tm,),
        in_specs=in_specs + [vec_spec, vec_spec],
        out_specs=[row_spec, row_spec],
        out_shape=[jax.ShapeDtypeStruct((n, d), F32), jax.ShapeDtypeStruct((n, d), BF16)],
        compiler_params=_params(("parallel",), vmem),
    )(*row_inputs, g.reshape(1, d), b.reshape(1, d))


def _mm_kernel(a_ref, b_ref, o_ref):
    o_ref[...] = jnp.dot(a_ref[...], b_ref[...], preferred_element_type=F32).astype(o_ref.dtype)


def _matmul(a, b, out_dtype, name):
    m, k = a.shape
    _, n = b.shape
    tn = _pick(n, (512, 256, 128))
    tm = _pick(m, (1024, 512, 256, 128)) if k <= 4096 else _pick(m, (512, 256, 128))
    vmem = 2 * (_nbytes((tm, k), BF16) + _nbytes((k, tn), BF16) + _nbytes((tm, tn), out_dtype)) + 2 * _nbytes((tm, tn), F32)
    return pl.pallas_call(
        _mm_kernel,
        grid=(m // tm, n // tn),
        in_specs=[pl.BlockSpec((tm, k), lambda i, j: (i, 0)), pl.BlockSpec((k, tn), lambda i, j: (0, j))],
        out_specs=pl.BlockSpec((tm, tn), lambda i, j: (i, j)),
        out_shape=jax.ShapeDtypeStruct((m, n), out_dtype),
        compiler_params=_params(("parallel", "arbitrary"), vmem),
        name=name,
    )(a, b)


def _mm_nt_kernel(w_ref, a_ref, o_ref):
    o_ref[...] = lax.dot_general(w_ref[...], a_ref[...], (((1,), (1,)), ((), ())),
                                 preferred_element_type=F32).astype(o_ref.dtype)


def _matmul_nt(w_t, a, out_dtype, name):
    f, k = w_t.shape
    m, _ = a.shape
    tm = _pick(m, (1024, 512, 256, 128))
    vmem = 2 * (_nbytes((tm, k), BF16) + _nbytes((f, k), BF16) + _nbytes((f, tm), out_dtype)) + 2 * _nbytes((f, tm), F32)
    return pl.pallas_call(
        _mm_nt_kernel,
        grid=(m // tm,),
        in_specs=[pl.BlockSpec((f, k), lambda i: (0, 0)), pl.BlockSpec((tm, k), lambda i: (i, 0))],
        out_specs=pl.BlockSpec((f, tm), lambda i: (0, i)),
        out_shape=jax.ShapeDtypeStruct((f, m), out_dtype),
        compiler_params=_params(("parallel",), vmem),
        name=name,
    )(w_t, a)


def _branch_mix_kernel(ya_ref, yb_ref, wa_ref, wb_ref, ga_ref, gb_ref, ba_ref, bb_ref, o_ref):
    acc_a = jnp.dot(ya_ref[...], wa_ref[...], preferred_element_type=F32)
    acc_b = jnp.dot(yb_ref[...], wb_ref[...], preferred_element_type=F32)
    gate_a = jax.nn.sigmoid(ga_ref[...].astype(F32) + ba_ref[...])
    gate_b = jax.nn.sigmoid(gb_ref[...].astype(F32) + bb_ref[...])
    o_ref[...] = (gate_a * acc_a + gate_b * acc_b).astype(o_ref.dtype)


def _branch_mix(ya, yb, wa, wb, gates, gate_bias, d):
    n, ka = ya.shape
    _, kb = yb.shape
    tm = _pick(n, (512, 256, 128))
    tn = _pick(d, (256, 128))
    nb = d // tn
    vmem = 2 * (_nbytes((tm, ka + kb), BF16) + _nbytes((ka + kb, tn), BF16) + 3 * _nbytes((tm, tn), BF16)) + 6 * _nbytes((tm, tn), F32)
    bias2 = gate_bias.reshape(1, 2 * d)
    return pl.pallas_call(
        _branch_mix_kernel,
        grid=(n // tm, nb),
        in_specs=[
            pl.BlockSpec((tm, ka), lambda i, j: (i, 0)),
            pl.BlockSpec((tm, kb), lambda i, j: (i, 0)),
            pl.BlockSpec((ka, tn), lambda i, j: (0, j)),
            pl.BlockSpec((kb, tn), lambda i, j: (0, j)),
            pl.BlockSpec((tm, tn), lambda i, j: (i, j)),
            pl.BlockSpec((tm, tn), lambda i, j: (i, j + nb)),
            pl.BlockSpec((1, tn), lambda i, j: (0, j)),
            pl.BlockSpec((1, tn), lambda i, j: (0, j + nb)),
        ],
        out_specs=pl.BlockSpec((tm, tn), lambda i, j: (i, j)),
        out_shape=jax.ShapeDtypeStruct((n, d), BF16),
        compiler_params=_params(("parallel", "arbitrary"), vmem),
        name="branch_mix",
    )(ya, yb, wa, wb, gates, gates, bias2, bias2)


def _split_dot(x, m_bf16):
    hi = x.astype(BF16)
    lo = (x - hi.astype(F32)).astype(BF16)
    return jnp.dot(jnp.concatenate([hi, lo], axis=1), m_bf16, preferred_element_type=F32)


def _cumsum_rows(tri, x):
    return jnp.dot(tri, x, preferred_element_type=F32, precision=lax.Precision.HIGHEST)


def _softplus(x):
    return jnp.maximum(x, 0.0) + jnp.log1p(jnp.exp(-jnp.abs(x)))


def _log_sigmoid(x):
    return -_softplus(-x)


def _silu(x):
    return x * jax.nn.sigmoid(x)


def _ssd_kernel(xs_ref, xsp_ref, bm_ref, bmp_ref, cm_ref, cmp_ref, z_ref, dt_ref,
                cwx_ref, cwb_ref, cwc_ref, cbx_ref, cbb_ref, cbc_ref,
                dtb_ref, alog_ref, dskip_ref, nw_ref, expand_ref,
                o_ref, state_ref, *, chunk, headdim, n_taps):
    c = pl.program_id(2)
    L = chunk
    first = c == 0

    @pl.when(first)
    def _():
        state_ref[...] = jnp.zeros_like(state_ref)

    row = lax.broadcasted_iota(jnp.int32, (L, 2 * L), 0)
    col = lax.broadcasted_iota(jnp.int32, (L, 2 * L), 1)
    keep_prev = jnp.logical_or(col >= L, jnp.logical_not(first))

    def shift_mat(k):
        hit = jnp.logical_and(col == row + (L - (n_taps - 1 - k)), keep_prev)
        return jnp.where(hit, 1.0, 0.0).astype(BF16)

    shifts = [shift_mat(k) for k in range(n_taps - 1)]

    def conv(cur_ref, prev_ref, w_ref, b_ref):
        cur = cur_ref[...]
        stack = jnp.concatenate([prev_ref[...], cur], axis=0)
        w = w_ref[...]
        acc = cur.astype(F32) * w[n_taps - 1:n_taps, :] + b_ref[...]
        for k in range(n_taps - 1):
            acc = acc + jnp.dot(shifts[k], stack, preferred_element_type=F32) * w[k:k + 1, :]
        return _silu(acc)

    xs = conv(xs_ref, xsp_ref, cwx_ref, cbx_ref)
    bmat = conv(bm_ref, bmp_ref, cwb_ref, cbb_ref)
    cmat = conv(cm_ref, cmp_ref, cwc_ref, cbc_ref)

    dt = _softplus(dt_ref[...] + dtb_ref[...])
    la = dt * (-jnp.exp(alog_ref[...]))
    r2 = lax.broadcasted_iota(jnp.int32, (L, L), 0)
    c2 = lax.broadcasted_iota(jnp.int32, (L, L), 1)
    causal = c2 <= r2
    tri = jnp.where(causal, 1.0, 0.0).astype(F32)
    acum = _cumsum_rows(tri, la)
    acum_t = acum.T
    a_last = acum[L - 1:L, :]

    expand = expand_ref[...]
    dt_e = _split_dot(dt, expand)
    eac_e = _split_dot(jnp.exp(acum), expand)
    wend_e = _split_dot(jnp.exp(a_last - acum), expand)

    xdt = xs * dt_e
    xdt_b = xdt.astype(BF16)
    cb = cmat.astype(BF16)
    bb = bmat.astype(BF16)
    scores = lax.dot_general(cb, bb, (((1,), (1,)), ((), ())), preferred_element_type=F32)

    gw = xs.shape[1]
    heads_per_tile = V7X_LANES // headdim
    lane = lax.broadcasted_iota(jnp.int32, (L, V7X_LANES), 1)
    y_tiles = []
    for q in range(gw // V7X_LANES):
        ms = []
        rhs = []
        x_tile = xdt_b[:, q * V7X_LANES:(q + 1) * V7X_LANES]
        for e in range(heads_per_tile):
            j = q * heads_per_tile + e
            seg = acum[:, j:j + 1] - acum_t[j:j + 1, :]
            ms.append((scores * jnp.where(causal, jnp.exp(seg), 0.0)).astype(BF16))
            in_head = jnp.logical_and(lane >= e * headdim, lane < (e + 1) * headdim)
            rhs.append(jnp.where(in_head, x_tile, jnp.zeros_like(x_tile)))
        y_tiles.append(jnp.dot(jnp.concatenate(ms, axis=1), jnp.concatenate(rhs, axis=0),
                               preferred_element_type=F32))
    y = jnp.concatenate(y_tiles, axis=1)

    state = state_ref[...]
    y = y + jnp.dot(cb, state.astype(BF16), preferred_element_type=F32) * eac_e
    y = y + xs * dskip_ref[...]
    y = y * _silu(z_ref[...].astype(F32))
    y = y * lax.rsqrt(jnp.mean(y * y, axis=-1, keepdims=True) + RMS_EPS)
    o_ref[...] = (y * nw_ref[...]).astype(o_ref.dtype)

    upd = jnp.dot(bmat.T.astype(BF16), (wend_e * xdt).astype(BF16), preferred_element_type=F32)
    state_ref[...] = state * eac_e[L - 1:L, :] + upd


def _ssd_branch(cfg, z, xbc, dt_pad, conv_w, conv_b, dt_bias, a_log, d_skip, norm_w, bsz, t_len):
    L = cfg.ssd_chunk
    G = cfg.ssd_groups
    GW = cfg.group_width
    N = cfg.ssd_state
    J = cfg.group_heads
    nc = t_len // L
    assert V7X_LANES % cfg.ssd_headdim == 0 and J <= V7X_LANES and GW % V7X_LANES == 0 and N % V7X_LANES == 0
    nx = cfg.d_inner // GW
    b_off = cfg.d_inner // N
    c_off = (cfg.d_inner + G * N) // N

    def pad_heads(v):
        return jnp.pad(v.reshape(G, 1, J), ((0, 0), (0, 0), (0, V7X_LANES - J)))

    dskip_e = jnp.repeat(d_skip, cfg.ssd_headdim).reshape(1, cfg.d_inner)
    head_of_chan = jnp.arange(GW) // cfg.ssd_headdim
    expand = (jnp.arange(V7X_LANES)[:, None] == head_of_chan[None, :]).astype(BF16)
    expand = jnp.concatenate([expand, expand], axis=0)

    cur = lambda b, g, c: b * nc + c
    prev = lambda b, g, c: b * nc + jnp.maximum(c - 1, 0)
    kernel = functools.partial(_ssd_kernel, chunk=L, headdim=cfg.ssd_headdim, n_taps=cfg.ssd_conv)
    vmem = 4 * _nbytes((L, GW), BF16) * 2 + 24 * _nbytes((L, GW), F32) + 2 * _nbytes((N, GW), F32)
    return pl.pallas_call(
        kernel,
        grid=(bsz, G, nc),
        in_specs=[
            pl.BlockSpec((L, GW), lambda b, g, c: (cur(b, g, c), g)),
            pl.BlockSpec((L, GW), lambda b, g, c: (prev(b, g, c), g)),
            pl.BlockSpec((L, N), lambda b, g, c: (cur(b, g, c), b_off + g)),
            pl.BlockSpec((L, N), lambda b, g, c: (prev(b, g, c), b_off + g)),
            pl.BlockSpec((L, N), lambda b, g, c: (cur(b, g, c), c_off + g)),
            pl.BlockSpec((L, N), lambda b, g, c: (prev(b, g, c), c_off + g)),
            pl.BlockSpec((L, GW), lambda b, g, c: (cur(b, g, c), g)),
            pl.BlockSpec((L, V7X_LANES), lambda b, g, c: (cur(b, g, c), g)),
            pl.BlockSpec((cfg.ssd_conv, GW), lambda b, g, c: (0, g)),
            pl.BlockSpec((cfg.ssd_conv, N), lambda b, g, c: (0, b_off + g)),
            pl.BlockSpec((cfg.ssd_conv, N), lambda b, g, c: (0, c_off + g)),
            pl.BlockSpec((1, GW), lambda b, g, c: (0, g)),
            pl.BlockSpec((1, N), lambda b, g, c: (0, b_off + g)),
            pl.BlockSpec((1, N), lambda b, g, c: (0, c_off + g)),
            pl.BlockSpec((None, 1, V7X_LANES), lambda b, g, c: (g, 0, 0)),
            pl.BlockSpec((None, 1, V7X_LANES), lambda b, g, c: (g, 0, 0)),
            pl.BlockSpec((1, GW), lambda b, g, c: (0, g)),
            pl.BlockSpec((1, GW), lambda b, g, c: (0, g)),
            pl.BlockSpec((2 * V7X_LANES, GW), lambda b, g, c: (0, 0)),
        ],
        out_specs=pl.BlockSpec((L, GW), lambda b, g, c: (cur(b, g, c), g)),
        out_shape=jax.ShapeDtypeStruct((bsz * t_len, cfg.d_inner), BF16),
        scratch_shapes=[pltpu.VMEM((N, GW), F32)],
        compiler_params=_params(("parallel", "parallel", "arbitrary"), vmem),
        name="ssd_scan",
    )(xbc, xbc, xbc, xbc, xbc, xbc, z, dt_pad,
      conv_w, conv_w, conv_w, conv_b.reshape(1, -1), conv_b.reshape(1, -1), conv_b.reshape(1, -1),
      pad_heads(dt_bias), pad_heads(a_log), dskip_e, norm_w.reshape(1, -1), expand)


def _mlstm_kernel(ib_ref, fb_ref, q_ref, k_ref, v_ref, o_ref, gc_ref, gr_ref, nw_ref,
                  out_ref, c_ref, n_ref, m_ref, *, chunk, n_heads, dk):
    h = pl.program_id(1)
    c = pl.program_id(2)
    L = chunk

    @pl.when(c == 0)
    def _():
        c_ref[...] = jnp.zeros_like(c_ref)
        n_ref[...] = jnp.zeros_like(n_ref)
        m_ref[...] = jnp.zeros_like(m_ref)

    ib = ib_ref[h]
    fb = fb_ref[h]
    gc = gc_ref[...]
    li_c = gc[:, 0:1] + ib
    lf_c = _log_sigmoid(gc[:, 1:2] + fb)
    gr = gr_ref[...]
    li_r = gr_ref[pl.ds(h, 1), :] + ib
    lf_r = _log_sigmoid(gr_ref[pl.ds(n_heads + h, 1), :] + fb)
    del gr

    r2 = lax.broadcasted_iota(jnp.int32, (L, L), 0)
    c2 = lax.broadcasted_iota(jnp.int32, (L, L), 1)
    causal = c2 <= r2
    tri = jnp.where(causal, 1.0, 0.0).astype(F32)
    tri_t = jnp.where(r2 <= c2, 1.0, 0.0).astype(F32)
    bcum_c = _cumsum_rows(tri, jnp.broadcast_to(lf_c, (L, V7X_LANES)))[:, 0:1]
    bcum_r = jnp.dot(jnp.broadcast_to(lf_r, (V7X_SUBLANES, L)), tri_t, preferred_element_type=F32,
                     precision=lax.Precision.HIGHEST)[0:1, :]
    b_last = bcum_c[L - 1:L, :]

    m_prev = m_ref[...]
    dlog = jnp.where(causal, bcum_c - bcum_r + li_r, -jnp.inf)
    inter_log = bcum_c + m_prev
    m_t = jnp.maximum(inter_log, jnp.max(dlog, axis=1, keepdims=True))
    w_intra = jnp.exp(dlog - m_t)
    w_inter = jnp.exp(inter_log - m_t)

    q = q_ref[...]
    k = k_ref[...]
    v = v_ref[...]
    scale = dk ** -0.5
    s = lax.dot_general(q, k, (((1,), (1,)), ((), ())), preferred_element_type=F32) * scale * w_intra
    c_st = c_ref[...]
    n_st = n_ref[...]
    qf = q.astype(F32) * scale
    num = jnp.dot(s.astype(BF16), v, preferred_element_type=F32)
    num = num + w_inter * (jnp.dot(q, c_st.astype(BF16), preferred_element_type=F32) * scale)
    den = jnp.sum(s, axis=1, keepdims=True) + w_inter * jnp.sum(qf * n_st, axis=1, keepdims=True)
    hid = num / jnp.maximum(jnp.abs(den), jnp.exp(-m_t))
    hid = hid * lax.rsqrt(jnp.mean(hid * hid, axis=-1, keepdims=True) + RMS_EPS)
    out_ref[...] = (hid * nw_ref[...] * jax.nn.sigmoid(o_ref[...].astype(F32))).astype(out_ref.dtype)

    end_log = b_last - bcum_c + li_c
    m_new = jnp.maximum(b_last + m_prev, jnp.max(end_log, axis=0, keepdims=True))
    w_end = jnp.exp(end_log - m_new)
    w_carry = jnp.exp(b_last + m_prev - m_new)
    kw = k.astype(F32) * w_end
    c_ref[...] = w_carry * c_st + jnp.dot(kw.T.astype(BF16), v, preferred_element_type=F32)
    n_ref[...] = w_carry * n_st + jnp.sum(kw, axis=0, keepdims=True)
    m_ref[...] = m_new


def _mlstm_branch(cfg, qk, v, o_pre, gate_cols, gate_rows, igate_b, fgate_b, norm_w, bsz, t_len):
    L = cfg.ml_chunk
    H = cfg.ml_heads
    dk, dv = cfg.ml_dk, cfg.ml_dv
    nc = t_len // L
    assert dk % V7X_LANES == 0 and dv % V7X_LANES == 0
    row = lambda b, h, c: b * nc + c
    kernel = functools.partial(_mlstm_kernel, chunk=L, n_heads=H, dk=dk)
    vmem = 4 * (2 * _nbytes((L, dk), BF16) + 3 * _nbytes((L, dv), BF16)) + 3 * _nbytes((dk, dv), F32) \
        + 10 * _nbytes((L, L), F32) + 8 * _nbytes((L, dv), F32)
    smem = pl.BlockSpec(memory_space=pltpu.SMEM)
    return pl.pallas_call(
        kernel,
        grid=(bsz, H, nc),
        in_specs=[
            smem, smem,
            pl.BlockSpec((L, dk), lambda b, h, c: (row(b, h, c), h)),
            pl.BlockSpec((L, dk), lambda b, h, c: (row(b, h, c), H + h)),
            pl.BlockSpec((L, dv), lambda b, h, c: (row(b, h, c), h)),
            pl.BlockSpec((L, dv), lambda b, h, c: (row(b, h, c), h)),
            pl.BlockSpec((L, V7X_LANES), lambda b, h, c: (row(b, h, c), h)),
            pl.BlockSpec((2 * H, L), lambda b, h, c: (0, row(b, h, c))),
            pl.BlockSpec((1, dv), lambda b, h, c: (0, h)),
        ],
        out_specs=pl.BlockSpec((L, dv), lambda b, h, c: (row(b, h, c), h)),
        out_shape=jax.ShapeDtypeStruct((bsz * t_len, cfg.ml_v), BF16),
        scratch_shapes=[pltpu.VMEM((dk, dv), F32), pltpu.VMEM((1, dk), F32), pltpu.VMEM((1, 1), F32)],
        compiler_params=_params(("parallel", "parallel", "arbitrary"), vmem),
        name="mlstm_scan",
    )(igate_b, fgate_b, qk, qk, v, o_pre, gate_cols, gate_rows, norm_w.reshape(1, -1))


def _peer_scores_kernel(q_ref, k1_ref, k2_ref, s_ref, *, half):
    q = q_ref[...]
    nt = (((1,), (1,)), ((), ()))
    s_ref[0] = lax.dot_general(k1_ref[...], q[:, :half], nt, preferred_element_type=F32)
    s_ref[1] = lax.dot_general(k2_ref[...], q[:, half:], nt, preferred_element_type=F32)


def _peer_scores(cfg, q, k1, k2):
    n = q.shape[0]
    H, nk, half = cfg.peer_heads, cfg.peer_nkeys, cfg.peer_half
    tm = _pick(n, (1024, 512, 256, 128))
    vmem = 4 * _nbytes((tm, 2 * half), BF16) + 4 * _nbytes((2, nk, tm), F32)
    return pl.pallas_call(
        functools.partial(_peer_scores_kernel, half=half),
        grid=(n // tm, H),
        in_specs=[
            pl.BlockSpec((tm, 2 * half), lambda i, h: (i, h)),
            pl.BlockSpec((None, nk, half), lambda i, h: (h, 0, 0)),
            pl.BlockSpec((None, nk, half), lambda i, h: (h, 0, 0)),
        ],
        out_specs=pl.BlockSpec((None, 2, nk, tm), lambda i, h: (h, 0, 0, i)),
        out_shape=jax.ShapeDtypeStruct((H, 2, nk, n), F32),
        compiler_params=_params(("parallel", "arbitrary"), vmem),
        name="peer_scores",
    )(q, k1, k2)


def _candidate_pairs(topk):
    return [(i, j) for i in range(topk) for j in range(topk) if (i + 1) * (j + 1) <= topk]


def _peer_stats_kernel(s_ref, tau_ref, mu_ref, *, nkeys, topk):
    shape = tau_ref.shape
    neg = jnp.full(shape, -jnp.inf, F32)

    def insert(lst, x):
        out = []
        for cur in lst:
            out.append(jnp.maximum(cur, x))
            x = jnp.minimum(cur, x)
        return tuple(out)

    def top_of_half(r):
        def body(kk, lst):
            return insert(lst, s_ref[r, kk])
        return lax.fori_loop(0, nkeys, body, tuple(neg for _ in range(topk)))

    a = top_of_half(0)
    b = top_of_half(1)
    cands = [a[i] + b[j] for (i, j) in _candidate_pairs(topk)]
    best = tuple(neg for _ in range(topk))
    for cnd in cands:
        best = insert(best, cnd)
    tau = best[topk - 1]
    m = best[0]
    z = jnp.zeros(shape, F32)
    for cnd in cands:
        z = z + jnp.where(cnd >= tau, jnp.exp(cnd - m), 0.0)
    tau_ref[...] = tau
    mu_ref[...] = m + jnp.log(z)


def _peer_stats(cfg, s):
    H, _, nk, n = s.shape
    rows = n // V7X_LANES
    tr = _pick(rows, (V7X_SUBLANES,))
    s5 = s.reshape(H, 2, nk, rows, V7X_LANES)
    vmem = 4 * _nbytes((2, nk, tr, V7X_LANES), F32)
    out = jax.ShapeDtypeStruct((H, rows, V7X_LANES), F32)
    tau, mu = pl.pallas_call(
        functools.partial(_peer_stats_kernel, nkeys=nk, topk=cfg.peer_topk),
        grid=(H, rows // tr),
        in_specs=[pl.BlockSpec((None, 2, nk, tr, V7X_LANES), lambda h, i: (h, 0, 0, i, 0))],
        out_specs=[pl.BlockSpec((None, tr, V7X_LANES), lambda h, i: (h, i, 0))] * 2,
        out_shape=[out, out],
        compiler_params=_params(("parallel", "parallel"), vmem),
        name="peer_stats",
    )(s5)
    return tau.reshape(H, n), mu.reshape(H, n)


def _peer_dense_kernel(x_ref, u_ref, vt_ref, s1_ref, s2_ref, tau_ref, mu_ref, yt_ref, *, n_heads, sub, nkeys):
    j = pl.program_id(1)
    st = lax.dot_general(u_ref[...], x_ref[...], (((1,), (1,)), ((), ())), preferred_element_type=F32)
    act = 0.5 * st * (1.0 + lax.erf(st * (2.0 ** -0.5)))
    parts = []
    for a in range(sub):
        i1 = j * sub + a
        w = None
        for h in range(n_heads):
            cnd = s1_ref[h, pl.ds(i1, 1), :] + s2_ref[h]
            wh = jnp.where(cnd >= tau_ref[pl.ds(h, 1), :], jnp.exp(cnd - mu_ref[pl.ds(h, 1), :]), 0.0)
            w = wh if w is None else w + wh
        parts.append((w * act[a * nkeys:(a + 1) * nkeys, :]).astype(BF16))
    weighted = parts[0] if sub == 1 else jnp.concatenate(parts, axis=0)
    contrib = jnp.dot(vt_ref[...], weighted, preferred_element_type=F32)

    @pl.when(j == 0)
    def _():
        yt_ref[...] = contrib

    @pl.when(j > 0)
    def _():
        yt_ref[...] += contrib


def _peer_dense(cfg, x_bf16, u_bf16, vt_bf16, s, tau, mu):
    n, d = x_bf16.shape
    H, nk = cfg.peer_heads, cfg.peer_nkeys
    E = cfg.peer_experts
    tm = _pick(n, (512, 256, 128))
    te = _pick(E, (256, 128))
    sub = te // nk
    assert nk == V7X_LANES and te % nk == 0
    vmem = 2 * (_nbytes((tm, d), BF16) + 2 * _nbytes((te, d), BF16) + 2 * _nbytes((H, nk, tm), F32)
                + _nbytes((d, tm), F32)) + _nbytes((d, tm), F32) + 8 * _nbytes((te, tm), F32)
    return pl.pallas_call(
        functools.partial(_peer_dense_kernel, n_heads=H, sub=sub, nkeys=nk),
        grid=(n // tm, E // te),
        in_specs=[
            pl.BlockSpec((tm, d), lambda i, j: (i, 0)),
            pl.BlockSpec((te, d), lambda i, j: (j, 0)),
            pl.BlockSpec((d, te), lambda i, j: (0, j)),
            pl.BlockSpec((H, None, nk, tm), lambda i, j: (0, 0, 0, i)),
            pl.BlockSpec((H, None, nk, tm), lambda i, j: (0, 1, 0, i)),
            pl.BlockSpec((H, tm), lambda i, j: (0, i)),
            pl.BlockSpec((H, tm), lambda i, j: (0, i)),
        ],
        out_specs=pl.BlockSpec((d, tm), lambda i, j: (0, i)),
        out_shape=jax.ShapeDtypeStruct((d, n), F32),
        compiler_params=_params(("parallel", "arbitrary"), vmem),
        name="peer_dense",
    )(x_bf16, u_bf16, vt_bf16, s, s, tau, mu)


def _mixer(cfg, h_bf16, bsz, t_len, w_in, conv_w, conv_b, dt_bias, a_log, d_skip, ssm_norm_w, w_branch_a,
           igate_b, fgate_b, mlstm_norm_w, w_branch_b, gate_b, w_out):
    d = cfg.d_model
    G, J, H = cfg.ssd_groups, cfg.group_heads, cfg.ml_heads
    splits = (cfg.d_inner, cfg.conv_dim, cfg.ssd_heads, cfg.ml_qk, cfg.ml_qk, cfg.ml_v, cfg.ml_v, H, H, d, d)
    offs = [0]
    for sp in splits:
        offs.append(offs[-1] + sp)
    seg = lambda a, b: w_in[:, offs[a]:offs[b]]
    w_z = seg(0, 1).astype(BF16)
    w_xbc = seg(1, 2).astype(BF16)
    w_dt = jnp.pad(seg(2, 3).reshape(d, G, J), ((0, 0), (0, 0), (0, V7X_LANES - J))).reshape(d, G * V7X_LANES).astype(BF16)
    w_qk = seg(3, 5).astype(BF16)
    w_v = seg(5, 6).astype(BF16)
    w_o = seg(6, 7).astype(BF16)
    w_i, w_f = seg(7, 8), seg(8, 9)
    w_gc = jnp.pad(jnp.stack([w_i, w_f], axis=-1), ((0, 0), (0, 0), (0, V7X_LANES - 2))).reshape(d, H * V7X_LANES).astype(BF16)
    w_gr = jnp.concatenate([w_i, w_f], axis=1).T.astype(BF16)
    w_g = seg(9, 11).astype(BF16)

    z = _matmul(h_bf16, w_z, BF16, "proj_z")
    xbc = _matmul(h_bf16, w_xbc, BF16, "proj_xbc")
    dt_pad = _matmul(h_bf16, w_dt, F32, "proj_dt")
    qk = _matmul(h_bf16, w_qk, BF16, "proj_qk")
    v = _matmul(h_bf16, w_v, BF16, "proj_v")
    o_pre = _matmul(h_bf16, w_o, BF16, "proj_o")
    gate_cols = _matmul(h_bf16, w_gc, F32, "proj_gate_cols")
    gate_rows = _matmul_nt(w_gr, h_bf16, F32, "proj_gate_rows")
    gates = _matmul(h_bf16, w_g, BF16, "proj_gates")

    y_a = _ssd_branch(cfg, z, xbc, dt_pad, conv_w, conv_b, dt_bias, a_log, d_skip, ssm_norm_w, bsz, t_len)
    y_b = _mlstm_branch(cfg, qk, v, o_pre, gate_cols, gate_rows, igate_b, fgate_b, mlstm_norm_w, bsz, t_len)
    mixed = _branch_mix(y_a, y_b, w_branch_a.astype(BF16), w_branch_b.astype(BF16), gates, gate_b, d)
    return _matmul(mixed, w_out.astype(BF16), F32, "proj_out")


def _peer(cfg, h_bf16, wq, k1, k2, u_tab, v_tab):
    q = _matmul(h_bf16, wq.astype(BF16), BF16, "peer_q")
    s = _peer_scores(cfg, q, k1.astype(BF16), k2.astype(BF16))
    tau, mu = _peer_stats(cfg, s)
    return _peer_dense(cfg, h_bf16, u_tab.astype(BF16), v_tab.T.astype(BF16), s, tau, mu)


def _forward(cfg, x, emb_ln_g, emb_ln_b, w_in, conv_w, conv_b, dt_bias, a_log, d_skip, ssm_norm_w,
             w_branch_a, igate_b, fgate_b, mlstm_norm_w, w_branch_b, gate_b, w_out, ln1_g, ln1_b,
             peer_wq, peer_k1, peer_k2, peer_u, peer_v, ln2_g, ln2_b):
    bsz, t_len, d = x.shape
    n = bsz * t_len
    alpha = (2 * cfg.depth) ** 0.25
    tm = _pick(n, (256, 128))
    h, hb = _ln_call(_ln_kernel, [x.reshape(n, d)], emb_ln_g, emb_ln_b, n, d, tm)
    for l in range(cfg.depth):
        mix = _mixer(cfg, hb, bsz, t_len, w_in[l], conv_w[l], conv_b[l], dt_bias[l], a_log[l], d_skip[l],
                     ssm_norm_w[l], w_branch_a[l], igate_b[l], fgate_b[l], mlstm_norm_w[l], w_branch_b[l],
                     gate_b[l], w_out[l])
        h, hb = _ln_call(functools.partial(_res_ln_kernel, alpha=alpha), [h, mix], ln1_g[l], ln1_b[l], n, d, tm)
        ffn_t = _peer(cfg, hb, peer_wq[l], peer_k1[l], peer_k2[l], peer_u[l], peer_v[l])
        h, hb = _ln_call(functools.partial(_res_ln_t_kernel, alpha=alpha), [h, ffn_t], ln2_g[l], ln2_b[l], n, d, tm,
                         transposed_last=True)
    return h.reshape(bsz, t_len, d)


def kernel(x, emb_ln_g, emb_ln_b, w_in, conv_w, conv_b, dt_bias, a_log, d_skip, ssm_norm_w, w_branch_a, igate_b,
           fgate_b, mlstm_norm_w, w_branch_b, gate_b, w_out, ln1_g, ln1_b, peer_wq, peer_k1, peer_k2, peer_u,
           peer_v, ln2_g, ln2_b):
    return _forward(Cfg(), x, emb_ln_g, emb_ln_b, w_in, conv_w, conv_b, dt_bias, a_log, d_skip, ssm_norm_w,
                    w_branch_a, igate_b, fgate_b, mlstm_norm_w, w_branch_b, gate_b, w_out, ln1_g, ln1_b,
                    peer_wq, peer_k1, peer_k2, peer_u, peer_v, ln2_g, ln2_b)
```

```python
import functools
import math
from typing import NamedTuple

import jax
import jax.numpy as jnp
from jax import lax
from jax.experimental import pallas as pl
from jax.experimental.pallas import tpu as pltpu

F32 = jnp.float32
BF16 = jnp.bfloat16

V7X_LANES = 128
V7X_SUBLANES = 8
V7X_VMEM_BYTES = 64 * 1024 * 1024
VMEM_BUDGET_BYTES = 56 * 1024 * 1024

STATS_UNROLL = 4

LN_EPS = 1e-5
RMS_EPS = 1e-6


class Cfg(NamedTuple):
    depth: int = 2
    d_model: int = 4096
    d_inner: int = 8192
    ssd_headdim: int = 64
    ssd_groups: int = 8
    ssd_state: int = 128
    ssd_conv: int = 4
    ssd_chunk: int = 128
    ml_heads: int = 8
    ml_qk: int = 2048
    ml_v: int = 4096
    ml_chunk: int = 256
    peer_heads: int = 8
    peer_nkeys: int = 128
    peer_dkey: int = 256
    peer_topk: int = 16

    @property
    def ssd_heads(self):
        return self.d_inner // self.ssd_headdim

    @property
    def group_heads(self):
        return self.ssd_heads // self.ssd_groups

    @property
    def group_width(self):
        return self.d_inner // self.ssd_groups

    @property
    def conv_dim(self):
        return self.d_inner + 2 * self.ssd_groups * self.ssd_state

    @property
    def ml_dk(self):
        return self.ml_qk // self.ml_heads

    @property
    def ml_dv(self):
        return self.ml_v // self.ml_heads

    @property
    def peer_half(self):
        return self.peer_dkey // 2

    @property
    def peer_experts(self):
        return self.peer_nkeys * self.peer_nkeys


def _params(semantics, vmem_bytes, flags=None):
    return pltpu.CompilerParams(dimension_semantics=semantics, flags=flags,
                                vmem_limit_bytes=int(min(max(vmem_bytes, 16 * 1024 * 1024), VMEM_BUDGET_BYTES)))


def _nbytes(shape, dtype):
    return math.prod(shape) * jnp.dtype(dtype).itemsize


def _pick(n, prefs):
    for p in prefs:
        if n % p == 0:
            return p
    return n


def _ln_rows(x, g, b):
    mu = jnp.mean(x, axis=-1, keepdims=True)
    xc = x - mu
    var = jnp.mean(xc * xc, axis=-1, keepdims=True)
    return xc * lax.rsqrt(var + LN_EPS) * g + b


def _ln_kernel(x_ref, g_ref, b_ref, of_ref, ob_ref):
    y = _ln_rows(x_ref[...], g_ref[...], b_ref[...])
    of_ref[...] = y
    ob_ref[...] = y.astype(BF16)


def _res_ln_kernel(h_ref, f_ref, g_ref, b_ref, of_ref, ob_ref, obt_ref, *, alpha):
    y = _ln_rows(alpha * h_ref[...] + f_ref[...], g_ref[...], b_ref[...])
    of_ref[...] = y
    ob_ref[...] = y.astype(BF16)
    obt_ref[...] = y.T.astype(BF16)


def _res_ln_t_kernel(h_ref, ft_ref, g_ref, b_ref, of_ref, ob_ref, *, alpha):
    y = _ln_rows(alpha * h_ref[...] + ft_ref[...].T, g_ref[...], b_ref[...])
    of_ref[...] = y
    ob_ref[...] = y.astype(BF16)


def _ln_call(kernel_fn, row_inputs, g, b, n, d, tm, transposed_last=False, emit_transposed=False):
    row_spec = pl.BlockSpec((tm, d), lambda i: (i, 0))
    col_spec = pl.BlockSpec((d, tm), lambda i: (0, i))
    vec_spec = pl.BlockSpec((1, d), lambda i: (0, 0))
    in_specs = [col_spec if (transposed_last and k == len(row_inputs) - 1) else row_spec
                for k in range(len(row_inputs))]
    out_specs = [row_spec, row_spec]
    out_shape = [jax.ShapeDtypeStruct((n, d), F32), jax.ShapeDtypeStruct((n, d), BF16)]
    if emit_transposed:
        out_specs.append(col_spec)
        out_shape.append(jax.ShapeDtypeStruct((d, n), BF16))
    vmem = 2 * (len(row_inputs) + 1) * _nbytes((tm, d), F32) + 4 * _nbytes((tm, d), BF16) + 5 * _nbytes((tm, d), F32)
    return pl.pallas_call(
        kernel_fn,
        grid=(n // tm,),
        in_specs=in_specs + [vec_spec, vec_spec],
        out_specs=out_specs,
        out_shape=out_shape,
        compiler_params=_params(("parallel",), vmem),
    )(*row_inputs, g.reshape(1, d), b.reshape(1, d))


def _mm_kernel(a_ref, b_ref, o_ref):
    o_ref[...] = jnp.dot(a_ref[...], b_ref[...], preferred_element_type=F32).astype(o_ref.dtype)


def _matmul(a, b, out_dtype, name):
    m, k = a.shape
    _, n = b.shape
    tn = _pick(n, (512, 256, 128))
    tm = _pick(m, (1024, 512, 256, 128)) if k <= 4096 else _pick(m, (512, 256, 128))
    vmem = 2 * (_nbytes((tm, k), BF16) + _nbytes((k, tn), BF16) + _nbytes((tm, tn), out_dtype)) + 2 * _nbytes((tm, tn), F32)
    return pl.pallas_call(
        _mm_kernel,
        grid=(m // tm, n // tn),
        in_specs=[pl.BlockSpec((tm, k), lambda i, j: (i, 0)), pl.BlockSpec((k, tn), lambda i, j: (0, j))],
        out_specs=pl.BlockSpec((tm, tn), lambda i, j: (i, j)),
        out_shape=jax.ShapeDtypeStruct((m, n), out_dtype),
        compiler_params=_params(("parallel", "arbitrary"), vmem),
        name=name,
    )(a, b)


def _mm_nt_kernel(w_ref, a_ref, o_ref):
    o_ref[...] = lax.dot_general(w_ref[...], a_ref[...], (((1,), (1,)), ((), ())),
                                 preferred_element_type=F32).astype(o_ref.dtype)


def _matmul_nt(w_t, a, out_dtype, name):
    f, k = w_t.shape
    m, _ = a.shape
    tm = _pick(m, (1024, 512, 256, 128))
    vmem = 2 * (_nbytes((tm, k), BF16) + _nbytes((f, k), BF16) + _nbytes((f, tm), out_dtype)) + 2 * _nbytes((f, tm), F32)
    return pl.pallas_call(
        _mm_nt_kernel,
        grid=(m // tm,),
        in_specs=[pl.BlockSpec((f, k), lambda i: (0, 0)), pl.BlockSpec((tm, k), lambda i: (i, 0))],
        out_specs=pl.BlockSpec((f, tm), lambda i: (0, i)),
        out_shape=jax.ShapeDtypeStruct((f, m), out_dtype),
        compiler_params=_params(("parallel",), vmem),
        name=name,
    )(w_t, a)


def _branch_mix_kernel(ya_ref, yb_ref, wa_ref, wb_ref, ga_ref, gb_ref, ba_ref, bb_ref, o_ref):
    acc_a = jnp.dot(ya_ref[...], wa_ref[...], preferred_element_type=F32)
    acc_b = jnp.dot(yb_ref[...], wb_ref[...], preferred_element_type=F32)
    gate_a = jax.nn.sigmoid(ga_ref[...].astype(F32) + ba_ref[...])
    gate_b = jax.nn.sigmoid(gb_ref[...].astype(F32) + bb_ref[...])
    o_ref[...] = (gate_a * acc_a + gate_b * acc_b).astype(o_ref.dtype)


def _branch_mix(ya, yb, wa, wb, gates, gate_bias, d):
    n, ka = ya.shape
    _, kb = yb.shape
    tm = _pick(n, (512, 256, 128))
    tn = _pick(d, (256, 128))
    nb = d // tn
    vmem = 2 * (_nbytes((tm, ka + kb), BF16) + _nbytes((ka + kb, tn), BF16) + 3 * _nbytes((tm, tn), BF16)) + 6 * _nbytes((tm, tn), F32)
    bias2 = gate_bias.reshape(1, 2 * d)
    return pl.pallas_call(
        _branch_mix_kernel,
        grid=(n // tm, nb),
        in_specs=[
            pl.BlockSpec((tm, ka), lambda i, j: (i, 0)),
            pl.BlockSpec((tm, kb), lambda i, j: (i, 0)),
            pl.BlockSpec((ka, tn), lambda i, j: (0, j)),
            pl.BlockSpec((kb, tn), lambda i, j: (0, j)),
            pl.BlockSpec((tm, tn), lambda i, j: (i, j)),
            pl.BlockSpec((tm, tn), lambda i, j: (i, j + nb)),
            pl.BlockSpec((1, tn), lambda i, j: (0, j)),
            pl.BlockSpec((1, tn), lambda i, j: (0, j + nb)),
        ],
        out_specs=pl.BlockSpec((tm, tn), lambda i, j: (i, j)),
        out_shape=jax.ShapeDtypeStruct((n, d), BF16),
        compiler_params=_params(("parallel", "arbitrary"), vmem),
        name="branch_mix",
    )(ya, yb, wa, wb, gates, gates, bias2, bias2)


def _split_dot(x, m_bf16):
    hi = x.astype(BF16)
    lo = (x - hi.astype(F32)).astype(BF16)
    return jnp.dot(jnp.concatenate([hi, lo], axis=1), m_bf16, preferred_element_type=F32)


def _cumsum_rows(tri, x):
    return jnp.dot(tri, x, preferred_element_type=F32, precision=lax.Precision.HIGHEST)


def _softplus(x):
    return jnp.maximum(x, 0.0) + jnp.log1p(jnp.exp(-jnp.abs(x)))


def _log_sigmoid(x):
    return -_softplus(-x)


def _silu(x):
    return x * jax.nn.sigmoid(x)


def _ssd_kernel(xs_ref, xsp_ref, bm_ref, bmp_ref, cm_ref, cmp_ref, z_ref, dt_ref,
                cwx_ref, cwb_ref, cwc_ref, cbx_ref, cbb_ref, cbc_ref,
                dtb_ref, alog_ref, dskip_ref, nw_ref, expand_ref,
                o_ref, state_ref, *, chunk, headdim, n_taps):
    c = pl.program_id(2)
    L = chunk
    first = c == 0

    @pl.when(first)
    def _():
        state_ref[...] = jnp.zeros_like(state_ref)

    row = lax.broadcasted_iota(jnp.int32, (L, 2 * L), 0)
    col = lax.broadcasted_iota(jnp.int32, (L, 2 * L), 1)
    keep_prev = jnp.logical_or(col >= L, jnp.logical_not(first))

    def shift_mat(k):
        hit = jnp.logical_and(col == row + (L - (n_taps - 1 - k)), keep_prev)
        return jnp.where(hit, 1.0, 0.0).astype(BF16)

    shifts = [shift_mat(k) for k in range(n_taps - 1)]

    def conv(cur_ref, prev_ref, w_ref, b_ref):
        cur = cur_ref[...]
        stack = jnp.concatenate([prev_ref[...], cur], axis=0)
        w = w_ref[...]
        acc = cur.astype(F32) * w[n_taps - 1:n_taps, :] + b_ref[...]
        for k in range(n_taps - 1):
            acc = acc + jnp.dot(shifts[k], stack, preferred_element_type=F32) * w[k:k + 1, :]
        return _silu(acc)

    xs = conv(xs_ref, xsp_ref, cwx_ref, cbx_ref)
    bmat = conv(bm_ref, bmp_ref, cwb_ref, cbb_ref)
    cmat = conv(cm_ref, cmp_ref, cwc_ref, cbc_ref)

    dt = _softplus(dt_ref[...] + dtb_ref[...])
    la = dt * (-jnp.exp(alog_ref[...]))
    r2 = lax.broadcasted_iota(jnp.int32, (L, L), 0)
    c2 = lax.broadcasted_iota(jnp.int32, (L, L), 1)
    causal = c2 <= r2
    tri = jnp.where(causal, 1.0, 0.0).astype(F32)
    acum = _cumsum_rows(tri, la)
    acum_t = acum.T
    a_last = acum[L - 1:L, :]

    expand = expand_ref[...]
    dt_e = _split_dot(dt, expand)
    eac_e = _split_dot(jnp.exp(acum), expand)
    wend_e = _split_dot(jnp.exp(a_last - acum), expand)

    xdt = xs * dt_e
    xdt_b = xdt.astype(BF16)
    cb = cmat.astype(BF16)
    bb = bmat.astype(BF16)
    scores = lax.dot_general(cb, bb, (((1,), (1,)), ((), ())), preferred_element_type=F32)

    gw = xs.shape[1]
    heads_per_tile = V7X_LANES // headdim
    lane = lax.broadcasted_iota(jnp.int32, (L, V7X_LANES), 1)
    y_tiles = []
    for q in range(gw // V7X_LANES):
        ms = []
        rhs = []
        x_tile = xdt_b[:, q * V7X_LANES:(q + 1) * V7X_LANES]
        for e in range(heads_per_tile):
            j = q * heads_per_tile + e
            seg = acum[:, j:j + 1] - acum_t[j:j + 1, :]
            ms.append((scores * jnp.where(causal, jnp.exp(seg), 0.0)).astype(BF16))
            in_head = jnp.logical_and(lane >= e * headdim, lane < (e + 1) * headdim)
            rhs.append(jnp.where(in_head, x_tile, jnp.zeros_like(x_tile)))
        y_tiles.append(jnp.dot(jnp.concatenate(ms, axis=1), jnp.concatenate(rhs, axis=0),
                               preferred_element_type=F32))
    y = jnp.concatenate(y_tiles, axis=1)

    state = state_ref[...]
    y = y + jnp.dot(cb, state.astype(BF16), preferred_element_type=F32) * eac_e
    y = y + xs * dskip_ref[...]
    y = y * _silu(z_ref[...].astype(F32))
    y = y * lax.rsqrt(jnp.mean(y * y, axis=-1, keepdims=True) + RMS_EPS)
    o_ref[...] = (y * nw_ref[...]).astype(o_ref.dtype)

    upd = jnp.dot(bmat.T.astype(BF16), (wend_e * xdt).astype(BF16), preferred_element_type=F32)
    state_ref[...] = state * eac_e[L - 1:L, :] + upd


def _ssd_branch(cfg, z, xbc, dt_pad, conv_w, conv_b, dt_bias, a_log, d_skip, norm_w, bsz, t_len):
    L = cfg.ssd_chunk
    G = cfg.ssd_groups
    GW = cfg.group_width
    N = cfg.ssd_state
    J = cfg.group_heads
    nc = t_len // L
    assert V7X_LANES % cfg.ssd_headdim == 0 and J <= V7X_LANES and GW % V7X_LANES == 0 and N % V7X_LANES == 0
    nx = cfg.d_inner // GW
    b_off = cfg.d_inner // N
    c_off = (cfg.d_inner + G * N) // N

    def pad_heads(v):
        return jnp.pad(v.reshape(G, 1, J), ((0, 0), (0, 0), (0, V7X_LANES - J)))

    dskip_e = jnp.repeat(d_skip, cfg.ssd_headdim).reshape(1, cfg.d_inner)
    head_of_chan = jnp.arange(GW) // cfg.ssd_headdim
    expand = (jnp.arange(V7X_LANES)[:, None] == head_of_chan[None, :]).astype(BF16)
    expand = jnp.concatenate([expand, expand], axis=0)

    cur = lambda b, g, c: b * nc + c
    prev = lambda b, g, c: b * nc + jnp.maximum(c - 1, 0)
    kernel = functools.partial(_ssd_kernel, chunk=L, headdim=cfg.ssd_headdim, n_taps=cfg.ssd_conv)
    vmem = 4 * _nbytes((L, GW), BF16) * 2 + 24 * _nbytes((L, GW), F32) + 2 * _nbytes((N, GW), F32)
    return pl.pallas_call(
        kernel,
        grid=(bsz, G, nc),
        in_specs=[
            pl.BlockSpec((L, GW), lambda b, g, c: (cur(b, g, c), g)),
            pl.BlockSpec((L, GW), lambda b, g, c: (prev(b, g, c), g)),
            pl.BlockSpec((L, N), lambda b, g, c: (cur(b, g, c), b_off + g)),
            pl.BlockSpec((L, N), lambda b, g, c: (prev(b, g, c), b_off + g)),
            pl.BlockSpec((L, N), lambda b, g, c: (cur(b, g, c), c_off + g)),
            pl.BlockSpec((L, N), lambda b, g, c: (prev(b, g, c), c_off + g)),
            pl.BlockSpec((L, GW), lambda b, g, c: (cur(b, g, c), g)),
            pl.BlockSpec((L, V7X_LANES), lambda b, g, c: (cur(b, g, c), g)),
            pl.BlockSpec((cfg.ssd_conv, GW), lambda b, g, c: (0, g)),
            pl.BlockSpec((cfg.ssd_conv, N), lambda b, g, c: (0, b_off + g)),
            pl.BlockSpec((cfg.ssd_conv, N), lambda b, g, c: (0, c_off + g)),
            pl.BlockSpec((1, GW), lambda b, g, c: (0, g)),
            pl.BlockSpec((1, N), lambda b, g, c: (0, b_off + g)),
            pl.BlockSpec((1, N), lambda b, g, c: (0, c_off + g)),
            pl.BlockSpec((None, 1, V7X_LANES), lambda b, g, c: (g, 0, 0)),
            pl.BlockSpec((None, 1, V7X_LANES), lambda b, g, c: (g, 0, 0)),
            pl.BlockSpec((1, GW), lambda b, g, c: (0, g)),
            pl.BlockSpec((1, GW), lambda b, g, c: (0, g)),
            pl.BlockSpec((2 * V7X_LANES, GW), lambda b, g, c: (0, 0)),
        ],
        out_specs=pl.BlockSpec((L, GW), lambda b, g, c: (cur(b, g, c), g)),
        out_shape=jax.ShapeDtypeStruct((bsz * t_len, cfg.d_inner), BF16),
        scratch_shapes=[pltpu.VMEM((N, GW), F32)],
        compiler_params=_params(("parallel", "parallel", "arbitrary"), vmem),
        name="ssd_scan",
    )(xbc, xbc, xbc, xbc, xbc, xbc, z, dt_pad,
      conv_w, conv_w, conv_w, conv_b.reshape(1, -1), conv_b.reshape(1, -1), conv_b.reshape(1, -1),
      pad_heads(dt_bias), pad_heads(a_log), dskip_e, norm_w.reshape(1, -1), expand)


def _mlstm_kernel(ib_ref, fb_ref, q_ref, k_ref, v_ref, o_ref, gc_ref, gr_ref, nw_ref,
                  out_ref, c_ref, n_ref, m_ref, *, chunk, n_heads, dk):
    h = pl.program_id(1)
    c = pl.program_id(2)
    L = chunk

    @pl.when(c == 0)
    def _():
        c_ref[...] = jnp.zeros_like(c_ref)
        n_ref[...] = jnp.zeros_like(n_ref)
        m_ref[...] = jnp.zeros_like(m_ref)

    ib = ib_ref[h]
    fb = fb_ref[h]
    gc = gc_ref[...]
    li_c = gc[:, 0:1] + ib
    lf_c = _log_sigmoid(gc[:, 1:2] + fb)
    gr = gr_ref[...]
    li_r = gr_ref[pl.ds(h, 1), :] + ib
    lf_r = _log_sigmoid(gr_ref[pl.ds(n_heads + h, 1), :] + fb)
    del gr

    r2 = lax.broadcasted_iota(jnp.int32, (L, L), 0)
    c2 = lax.broadcasted_iota(jnp.int32, (L, L), 1)
    causal = c2 <= r2
    tri = jnp.where(causal, 1.0, 0.0).astype(F32)
    tri_t = jnp.where(r2 <= c2, 1.0, 0.0).astype(F32)
    bcum_c = _cumsum_rows(tri, jnp.broadcast_to(lf_c, (L, V7X_LANES)))[:, 0:1]
    bcum_r = jnp.dot(jnp.broadcast_to(lf_r, (V7X_SUBLANES, L)), tri_t, preferred_element_type=F32,
                     precision=lax.Precision.HIGHEST)[0:1, :]
    b_last = bcum_c[L - 1:L, :]

    m_prev = m_ref[...]
    dlog = jnp.where(causal, bcum_c - bcum_r + li_r, -jnp.inf)
    inter_log = bcum_c + m_prev
    m_t = jnp.maximum(inter_log, jnp.max(dlog, axis=1, keepdims=True))
    w_intra = jnp.exp(dlog - m_t)
    w_inter = jnp.exp(inter_log - m_t)

    q = q_ref[...]
    k = k_ref[...]
    v = v_ref[...]
    scale = dk ** -0.5
    s = lax.dot_general(q, k, (((1,), (1,)), ((), ())), preferred_element_type=F32) * scale * w_intra
    c_st = c_ref[...]
    n_st = n_ref[...]
    qf = q.astype(F32) * scale
    num = jnp.dot(s.astype(BF16), v, preferred_element_type=F32)
    num = num + w_inter * (jnp.dot(q, c_st.astype(BF16), preferred_element_type=F32) * scale)
    den = jnp.sum(s, axis=1, keepdims=True) + w_inter * jnp.sum(qf * n_st, axis=1, keepdims=True)
    hid = num / jnp.maximum(jnp.abs(den), jnp.exp(-m_t))
    hid = hid * lax.rsqrt(jnp.mean(hid * hid, axis=-1, keepdims=True) + RMS_EPS)
    out_ref[...] = (hid * nw_ref[...] * jax.nn.sigmoid(o_ref[...].astype(F32))).astype(out_ref.dtype)

    end_log = b_last - bcum_c + li_c
    m_new = jnp.maximum(b_last + m_prev, jnp.max(end_log, axis=0, keepdims=True))
    w_end = jnp.exp(end_log - m_new)
    w_carry = jnp.exp(b_last + m_prev - m_new)
    kw = k.astype(F32) * w_end
    c_ref[...] = w_carry * c_st + jnp.dot(kw.T.astype(BF16), v, preferred_element_type=F32)
    n_ref[...] = w_carry * n_st + jnp.sum(kw, axis=0, keepdims=True)
    m_ref[...] = m_new


def _mlstm_branch(cfg, qk, v, o_pre, gate_cols, gate_rows, igate_b, fgate_b, norm_w, bsz, t_len):
    L = cfg.ml_chunk
    H = cfg.ml_heads
    dk, dv = cfg.ml_dk, cfg.ml_dv
    nc = t_len // L
    assert dk % V7X_LANES == 0 and dv % V7X_LANES == 0
    row = lambda b, h, c: b * nc + c
    kernel = functools.partial(_mlstm_kernel, chunk=L, n_heads=H, dk=dk)
    vmem = 4 * (2 * _nbytes((L, dk), BF16) + 3 * _nbytes((L, dv), BF16)) + 3 * _nbytes((dk, dv), F32) \
        + 10 * _nbytes((L, L), F32) + 8 * _nbytes((L, dv), F32)
    smem = pl.BlockSpec(memory_space=pltpu.SMEM)
    return pl.pallas_call(
        kernel,
        grid=(bsz, H, nc),
        in_specs=[
            smem, smem,
            pl.BlockSpec((L, dk), lambda b, h, c: (row(b, h, c), h)),
            pl.BlockSpec((L, dk), lambda b, h, c: (row(b, h, c), H + h)),
            pl.BlockSpec((L, dv), lambda b, h, c: (row(b, h, c), h)),
            pl.BlockSpec((L, dv), lambda b, h, c: (row(b, h, c), h)),
            pl.BlockSpec((L, V7X_LANES), lambda b, h, c: (row(b, h, c), h)),
            pl.BlockSpec((2 * H, L), lambda b, h, c: (0, row(b, h, c))),
            pl.BlockSpec((1, dv), lambda b, h, c: (0, h)),
        ],
        out_specs=pl.BlockSpec((L, dv), lambda b, h, c: (row(b, h, c), h)),
        out_shape=jax.ShapeDtypeStruct((bsz * t_len, cfg.ml_v), BF16),
        scratch_shapes=[pltpu.VMEM((dk, dv), F32), pltpu.VMEM((1, dk), F32), pltpu.VMEM((1, 1), F32)],
        compiler_params=_params(("parallel", "parallel", "arbitrary"), vmem),
        name="mlstm_scan",
    )(igate_b, fgate_b, qk, qk, v, o_pre, gate_cols, gate_rows, norm_w.reshape(1, -1))


def _peer_scores_kernel(q_ref, k1_ref, k2_ref, s_ref, *, half):
    q = q_ref[...]
    nt = (((1,), (1,)), ((), ()))
    s_ref[0] = lax.dot_general(k1_ref[...], q[:, :half], nt, preferred_element_type=F32)
    s_ref[1] = lax.dot_general(k2_ref[...], q[:, half:], nt, preferred_element_type=F32)


def _peer_scores(cfg, q, k1, k2):
    n = q.shape[0]
    H, nk, half = cfg.peer_heads, cfg.peer_nkeys, cfg.peer_half
    tm = _pick(n, (1024, 512, 256, 128))
    vmem = 4 * _nbytes((tm, 2 * half), BF16) + 4 * _nbytes((2, nk, tm), F32)
    return pl.pallas_call(
        functools.partial(_peer_scores_kernel, half=half),
        grid=(n // tm, H),
        in_specs=[
            pl.BlockSpec((tm, 2 * half), lambda i, h: (i, h)),
            pl.BlockSpec((None, nk, half), lambda i, h: (h, 0, 0)),
            pl.BlockSpec((None, nk, half), lambda i, h: (h, 0, 0)),
        ],
        out_specs=pl.BlockSpec((None, 2, nk, tm), lambda i, h: (h, 0, 0, i)),
        out_shape=jax.ShapeDtypeStruct((H, 2, nk, n), F32),
        compiler_params=_params(("parallel", "arbitrary"), vmem),
        name="peer_scores",
    )(q, k1, k2)


def _candidate_pairs(topk):
    return [(i, j) for i in range(topk) for j in range(topk) if (i + 1) * (j + 1) <= topk]


def _peer_stats_kernel(s_ref, tau_ref, mu_ref, *, nkeys, topk):
    shape = tau_ref.shape
    neg = jnp.full(shape, -jnp.inf, F32)

    def insert(lst, x):
        out = []
        for cur in lst:
            out.append(jnp.maximum(cur, x))
            x = jnp.minimum(cur, x)
        return tuple(out)

    def top_of_half(r):
        def body(kk, lst):
            return insert(lst, s_ref[r, kk])
        return lax.fori_loop(0, nkeys, body, tuple(neg for _ in range(topk)), unroll=STATS_UNROLL)

    a = top_of_half(0)
    b = top_of_half(1)
    cands = [a[i] + b[j] for (i, j) in _candidate_pairs(topk)]
    best = tuple(neg for _ in range(topk))
    for cnd in cands:
        best = insert(best, cnd)
    tau = best[topk - 1]
    m = best[0]
    z = jnp.zeros(shape, F32)
    for cnd in cands:
        z = z + jnp.where(cnd >= tau, jnp.exp(cnd - m), 0.0)
    tau_ref[...] = tau
    mu_ref[...] = m + jnp.log(z)


def _peer_stats(cfg, s):
    H, _, nk, n = s.shape
    rows = n // V7X_LANES
    tr = _pick(rows, (2 * V7X_SUBLANES, V7X_SUBLANES))
    s5 = s.reshape(H, 2, nk, rows, V7X_LANES)
    vmem = 4 * _nbytes((2, nk, tr, V7X_LANES), F32)
    out = jax.ShapeDtypeStruct((H, rows, V7X_LANES), F32)
    tau, mu = pl.pallas_call(
        functools.partial(_peer_stats_kernel, nkeys=nk, topk=cfg.peer_topk),
        grid=(H, rows // tr),
        in_specs=[pl.BlockSpec((None, 2, nk, tr, V7X_LANES), lambda h, i: (h, 0, 0, i, 0))],
        out_specs=[pl.BlockSpec((None, tr, V7X_LANES), lambda h, i: (h, i, 0))] * 2,
        out_shape=[out, out],
        compiler_params=_params(("parallel", "parallel"), vmem),
        name="peer_stats",
    )(s5)
    return tau.reshape(H, n), mu.reshape(H, n)


def _peer_dense_kernel(xt_ref, u_ref, vt_ref, s1_ref, s2_ref, tau_ref, mu_ref, yt_ref, st_scr, w_scr,
                       *, n_heads, sub, nkeys, pieces, d_chunk):
    j = pl.program_id(1)

    @pl.when(j == 0)
    def _():
        yt_ref[...] = jnp.zeros_like(yt_ref)

    rows_p = sub * nkeys // pieces
    sub_p = sub // pieces
    xt = xt_ref[...]
    for p in range(pieces):
        st_scr[p] = jnp.dot(u_ref[p * rows_p:(p + 1) * rows_p, :], xt, preferred_element_type=F32)
    for p in range(pieces):
        for a in range(sub_p):
            i1 = j * sub + p * sub_p + a
            w = None
            for h in range(n_heads):
                cnd = s1_ref[h, pl.ds(i1, 1), :] + s2_ref[h]
                wh = jnp.where(cnd >= tau_ref[pl.ds(h, 1), :], jnp.exp(cnd - mu_ref[pl.ds(h, 1), :]), 0.0)
                w = wh if w is None else w + wh
            st = st_scr[p, a * nkeys:(a + 1) * nkeys, :]
            act = 0.5 * st * (1.0 + lax.erf(st * (2.0 ** -0.5)))
            w_scr[p, a * nkeys:(a + 1) * nkeys, :] = (w * act).astype(BF16)
        for dc in range(yt_ref.shape[0] // d_chunk):
            rows = pl.ds(dc * d_chunk, d_chunk)
            yt_ref[rows, :] += jnp.dot(vt_ref[rows, p * rows_p:(p + 1) * rows_p], w_scr[p],
                                       preferred_element_type=F32)


def _peer_dense(cfg, xt_bf16, u_bf16, vt_bf16, s, tau, mu):
    d, n = xt_bf16.shape
    H, nk = cfg.peer_heads, cfg.peer_nkeys
    E = cfg.peer_experts
    tm = _pick(n, (512, 256, 128))
    te = _pick(E, (512, 256, 128))
    d_chunk = _pick(d, (512, 256, 128))
    sub = te // nk
    pieces = 2 if sub % 2 == 0 else 1
    assert nk == V7X_LANES and te % nk == 0
    once = pl.Buffered(1)
    vmem = (_nbytes((tm, d), BF16) + 4 * _nbytes((te, d), BF16) + 2 * _nbytes((H, nk, tm), F32)
            + 2 * _nbytes((d, tm), F32) + 12 * _nbytes((te, tm), F32) + 2 * _nbytes((d_chunk, tm), F32))
    return pl.pallas_call(
        functools.partial(_peer_dense_kernel, n_heads=H, sub=sub, nkeys=nk, pieces=pieces, d_chunk=d_chunk),
        grid=(n // tm, E // te),
        in_specs=[
            pl.BlockSpec((d, tm), lambda i, j: (0, i), pipeline_mode=once),
            pl.BlockSpec((te, d), lambda i, j: (j, 0)),
            pl.BlockSpec((d, te), lambda i, j: (0, j)),
            pl.BlockSpec((H, None, nk, tm), lambda i, j: (0, 0, 0, i), pipeline_mode=once),
            pl.BlockSpec((H, None, nk, tm), lambda i, j: (0, 1, 0, i), pipeline_mode=once),
            pl.BlockSpec((H, tm), lambda i, j: (0, i), pipeline_mode=once),
            pl.BlockSpec((H, tm), lambda i, j: (0, i), pipeline_mode=once),
        ],
        out_specs=pl.BlockSpec((d, tm), lambda i, j: (0, i)),
        out_shape=jax.ShapeDtypeStruct((d, n), F32),
        scratch_shapes=[pltpu.VMEM((pieces, te // pieces, tm), F32), pltpu.VMEM((pieces, te // pieces, tm), BF16)],
        compiler_params=_params(("parallel", "arbitrary"), vmem),
        name="peer_dense",
    )(xt_bf16, u_bf16, vt_bf16, s, s, tau, mu)


def _mixer(cfg, h_bf16, bsz, t_len, w_in, conv_w, conv_b, dt_bias, a_log, d_skip, ssm_norm_w, w_branch_a,
           igate_b, fgate_b, mlstm_norm_w, w_branch_b, gate_b, w_out):
    d = cfg.d_model
    G, J, H = cfg.ssd_groups, cfg.group_heads, cfg.ml_heads
    splits = (cfg.d_inner, cfg.conv_dim, cfg.ssd_heads, cfg.ml_qk, cfg.ml_qk, cfg.ml_v, cfg.ml_v, H, H, d, d)
    offs = [0]
    for sp in splits:
        offs.append(offs[-1] + sp)
    seg = lambda a, b: w_in[:, offs[a]:offs[b]]
    w_z = seg(0, 1).astype(BF16)
    w_xbc = seg(1, 2).astype(BF16)
    w_dt = jnp.pad(seg(2, 3).reshape(d, G, J), ((0, 0), (0, 0), (0, V7X_LANES - J))).reshape(d, G * V7X_LANES).astype(BF16)
    w_qk = seg(3, 5).astype(BF16)
    w_v = seg(5, 6).astype(BF16)
    w_o = seg(6, 7).astype(BF16)
    w_i, w_f = seg(7, 8), seg(8, 9)
    w_gc = jnp.pad(jnp.stack([w_i, w_f], axis=-1), ((0, 0), (0, 0), (0, V7X_LANES - 2))).reshape(d, H * V7X_LANES).astype(BF16)
    w_gr = jnp.concatenate([w_i, w_f], axis=1).T.astype(BF16)
    w_g = seg(9, 11).astype(BF16)

    z = _matmul(h_bf16, w_z, BF16, "proj_z")
    xbc = _matmul(h_bf16, w_xbc, BF16, "proj_xbc")
    dt_pad = _matmul(h_bf16, w_dt, F32, "proj_dt")
    qk = _matmul(h_bf16, w_qk, BF16, "proj_qk")
    v = _matmul(h_bf16, w_v, BF16, "proj_v")
    o_pre = _matmul(h_bf16, w_o, BF16, "proj_o")
    gate_cols = _matmul(h_bf16, w_gc, F32, "proj_gate_cols")
    gate_rows = _matmul_nt(w_gr, h_bf16, F32, "proj_gate_rows")
    gates = _matmul(h_bf16, w_g, BF16, "proj_gates")

    y_a = _ssd_branch(cfg, z, xbc, dt_pad, conv_w, conv_b, dt_bias, a_log, d_skip, ssm_norm_w, bsz, t_len)
    y_b = _mlstm_branch(cfg, qk, v, o_pre, gate_cols, gate_rows, igate_b, fgate_b, mlstm_norm_w, bsz, t_len)
    mixed = _branch_mix(y_a, y_b, w_branch_a.astype(BF16), w_branch_b.astype(BF16), gates, gate_b, d)
    return _matmul(mixed, w_out.astype(BF16), F32, "proj_out")


def _peer(cfg, h_bf16, ht_bf16, wq, k1, k2, u_tab, v_tab):
    q = _matmul(h_bf16, wq.astype(BF16), BF16, "peer_q")
    s = _peer_scores(cfg, q, k1.astype(BF16), k2.astype(BF16))
    tau, mu = _peer_stats(cfg, s)
    return _peer_dense(cfg, ht_bf16, u_tab.astype(BF16), v_tab.T.astype(BF16), s, tau, mu)


def _forward(cfg, x, emb_ln_g, emb_ln_b, w_in, conv_w, conv_b, dt_bias, a_log, d_skip, ssm_norm_w,
             w_branch_a, igate_b, fgate_b, mlstm_norm_w, w_branch_b, gate_b, w_out, ln1_g, ln1_b,
             peer_wq, peer_k1, peer_k2, peer_u, peer_v, ln2_g, ln2_b):
    bsz, t_len, d = x.shape
    n = bsz * t_len
    alpha = (2 * cfg.depth) ** 0.25
    tm = _pick(n, (256, 128))
    h, hb = _ln_call(_ln_kernel, [x.reshape(n, d)], emb_ln_g, emb_ln_b, n, d, tm)
    for l in range(cfg.depth):
        mix = _mixer(cfg, hb, bsz, t_len, w_in[l], conv_w[l], conv_b[l], dt_bias[l], a_log[l], d_skip[l],
                     ssm_norm_w[l], w_branch_a[l], igate_b[l], fgate_b[l], mlstm_norm_w[l], w_branch_b[l],
                     gate_b[l], w_out[l])
        h, hb, hbt = _ln_call(functools.partial(_res_ln_kernel, alpha=alpha), [h, mix], ln1_g[l], ln1_b[l], n, d, tm,
                              emit_transposed=True)
        ffn_t = _peer(cfg, hb, hbt, peer_wq[l], peer_k1[l], peer_k2[l], peer_u[l], peer_v[l])
        h, hb = _ln_call(functools.partial(_res_ln_t_kernel, alpha=alpha), [h, ffn_t], ln2_g[l], ln2_b[l], n, d, tm,
                         transposed_last=True)
    return h.reshape(bsz, t_len, d)


def kernel(x, emb_ln_g, emb_ln_b, w_in, conv_w, conv_b, dt_bias, a_log, d_skip, ssm_norm_w, w_branch_a, igate_b,
           fgate_b, mlstm_norm_w, w_branch_b, gate_b, w_out, ln1_g, ln1_b, peer_wq, peer_k1, peer_k2, peer_u,
           peer_v, ln2_g, ln2_b):
    return _forward(Cfg(), x, emb_ln_g, emb_ln_b, w_in, conv_w, conv_b, dt_bias, a_log, d_skip, ssm_norm_w,
                    w_branch_a, igate_b, fgate_b, mlstm_norm_w, w_branch_b, gate_b, w_out, ln1_g, ln1_b,
                    peer_wq, peer_k1, peer_k2, peer_u, peer_v, ln2_g, ln2_b)
```

```python
import functools
import math
from typing import NamedTuple

import jax
import jax.numpy as jnp
from jax import lax
from jax.experimental import pallas as pl
from jax.experimental.pallas import tpu as pltpu

F32 = jnp.float32
BF16 = jnp.bfloat16

V7X_LANES = 128
V7X_SUBLANES = 8
V7X_VMEM_BYTES = 64 * 1024 * 1024
VMEM_BUDGET_BYTES = 56 * 1024 * 1024

STATS_UNROLL = 4

LN_EPS = 1e-5
RMS_EPS = 1e-6


class Cfg(NamedTuple):
    depth: int = 2
    d_model: int = 4096
    d_inner: int = 8192
    ssd_headdim: int = 64
    ssd_groups: int = 8
    ssd_state: int = 128
    ssd_conv: int = 4
    ssd_chunk: int = 128
    ml_heads: int = 8
    ml_qk: int = 2048
    ml_v: int = 4096
    ml_chunk: int = 256
    peer_heads: int = 8
    peer_nkeys: int = 128
    peer_dkey: int = 256
    peer_topk: int = 16

    @property
    def ssd_heads(self):
        return self.d_inner // self.ssd_headdim

    @property
    def group_heads(self):
        return self.ssd_heads // self.ssd_groups

    @property
    def group_width(self):
        return self.d_inner // self.ssd_groups

    @property
    def conv_dim(self):
        return self.d_inner + 2 * self.ssd_groups * self.ssd_state

    @property
    def ml_dk(self):
        return self.ml_qk // self.ml_heads

    @property
    def ml_dv(self):
        return self.ml_v // self.ml_heads

    @property
    def peer_half(self):
        return self.peer_dkey // 2

    @property
    def peer_experts(self):
        return self.peer_nkeys * self.peer_nkeys


def _params(semantics, vmem_bytes, flags=None):
    return pltpu.CompilerParams(dimension_semantics=semantics, flags=flags,
                                vmem_limit_bytes=int(min(max(vmem_bytes, 16 * 1024 * 1024), VMEM_BUDGET_BYTES)))


def _nbytes(shape, dtype):
    return math.prod(shape) * jnp.dtype(dtype).itemsize


def _pick(n, prefs):
    for p in prefs:
        if n % p == 0:
            return p
    return n


def _ln_rows(x, g, b):
    mu = jnp.mean(x, axis=-1, keepdims=True)
    xc = x - mu
    var = jnp.mean(xc * xc, axis=-1, keepdims=True)
    return xc * lax.rsqrt(var + LN_EPS) * g + b


def _ln_kernel(x_ref, g_ref, b_ref, of_ref, ob_ref):
    y = _ln_rows(x_ref[...], g_ref[...], b_ref[...])
    of_ref[...] = y
    ob_ref[...] = y.astype(BF16)


def _res_ln_kernel(h_ref, f_ref, g_ref, b_ref, of_ref, ob_ref, obt_ref, *, alpha):
    y = _ln_rows(alpha * h_ref[...] + f_ref[...], g_ref[...], b_ref[...])
    of_ref[...] = y
    ob_ref[...] = y.astype(BF16)
    obt_ref[...] = y.T.astype(BF16)


def _res_ln_t_kernel(h_ref, ft_ref, g_ref, b_ref, of_ref, ob_ref, *, alpha):
    y = _ln_rows(alpha * h_ref[...] + ft_ref[...].T, g_ref[...], b_ref[...])
    of_ref[...] = y
    ob_ref[...] = y.astype(BF16)


def _ln_call(kernel_fn, row_inputs, g, b, n, d, tm, transposed_last=False, emit_transposed=False):
    row_spec = pl.BlockSpec((tm, d), lambda i: (i, 0))
    col_spec = pl.BlockSpec((d, tm), lambda i: (0, i))
    vec_spec = pl.BlockSpec((1, d), lambda i: (0, 0))
    in_specs = [col_spec if (transposed_last and k == len(row_inputs) - 1) else row_spec
                for k in range(len(row_inputs))]
    out_specs = [row_spec, row_spec]
    out_shape = [jax.ShapeDtypeStruct((n, d), F32), jax.ShapeDtypeStruct((n, d), BF16)]
    if emit_transposed:
        out_specs.append(col_spec)
        out_shape.append(jax.ShapeDtypeStruct((d, n), BF16))
    vmem = 2 * (len(row_inputs) + 1) * _nbytes((tm, d), F32) + 4 * _nbytes((tm, d), BF16) + 5 * _nbytes((tm, d), F32)
    return pl.pallas_call(
        kernel_fn,
        grid=(n // tm,),
        in_specs=in_specs + [vec_spec, vec_spec],
        out_specs=out_specs,
        out_shape=out_shape,
        compiler_params=_params(("parallel",), vmem),
    )(*row_inputs, g.reshape(1, d), b.reshape(1, d))


def _mm_kernel(a_ref, b_ref, o_ref):
    o_ref[...] = jnp.dot(a_ref[...], b_ref[...], preferred_element_type=F32).astype(o_ref.dtype)


def _matmul(a, b, out_dtype, name):
    m, k = a.shape
    _, n = b.shape
    tn = _pick(n, (512, 256, 128))
    tm = _pick(m, (1024, 512, 256, 128)) if k <= 4096 else _pick(m, (512, 256, 128))
    vmem = 2 * (_nbytes((tm, k), BF16) + _nbytes((k, tn), BF16) + _nbytes((tm, tn), out_dtype)) + 2 * _nbytes((tm, tn), F32)
    return pl.pallas_call(
        _mm_kernel,
        grid=(m // tm, n // tn),
        in_specs=[pl.BlockSpec((tm, k), lambda i, j: (i, 0)), pl.BlockSpec((k, tn), lambda i, j: (0, j))],
        out_specs=pl.BlockSpec((tm, tn), lambda i, j: (i, j)),
        out_shape=jax.ShapeDtypeStruct((m, n), out_dtype),
        compiler_params=_params(("parallel", "arbitrary"), vmem),
        name=name,
    )(a, b)


def _mm_nt_kernel(w_ref, a_ref, o_ref):
    o_ref[...] = lax.dot_general(w_ref[...], a_ref[...], (((1,), (1,)), ((), ())),
                                 preferred_element_type=F32).astype(o_ref.dtype)


def _matmul_nt(w_t, a, out_dtype, name):
    f, k = w_t.shape
    m, _ = a.shape
    tm = _pick(m, (1024, 512, 256, 128))
    vmem = 2 * (_nbytes((tm, k), BF16) + _nbytes((f, k), BF16) + _nbytes((f, tm), out_dtype)) + 2 * _nbytes((f, tm), F32)
    return pl.pallas_call(
        _mm_nt_kernel,
        grid=(m // tm,),
        in_specs=[pl.BlockSpec((f, k), lambda i: (0, 0)), pl.BlockSpec((tm, k), lambda i: (i, 0))],
        out_specs=pl.BlockSpec((f, tm), lambda i: (0, i)),
        out_shape=jax.ShapeDtypeStruct((f, m), out_dtype),
        compiler_params=_params(("parallel",), vmem),
        name=name,
    )(w_t, a)


def _branch_mix_kernel(ya_ref, yb_ref, wa_ref, wb_ref, ga_ref, gb_ref, ba_ref, bb_ref, o_ref):
    acc_a = jnp.dot(ya_ref[...], wa_ref[...], preferred_element_type=F32)
    acc_b = jnp.dot(yb_ref[...], wb_ref[...], preferred_element_type=F32)
    gate_a = jax.nn.sigmoid(ga_ref[...].astype(F32) + ba_ref[...])
    gate_b = jax.nn.sigmoid(gb_ref[...].astype(F32) + bb_ref[...])
    o_ref[...] = (gate_a * acc_a + gate_b * acc_b).astype(o_ref.dtype)


def _branch_mix(ya, yb, wa, wb, gates, gate_bias, d):
    n, ka = ya.shape
    _, kb = yb.shape
    tm = _pick(n, (512, 256, 128))
    tn = _pick(d, (256, 128))
    nb = d // tn
    vmem = 2 * (_nbytes((tm, ka + kb), BF16) + _nbytes((ka + kb, tn), BF16) + 3 * _nbytes((tm, tn), BF16)) + 6 * _nbytes((tm, tn), F32)
    bias2 = gate_bias.reshape(1, 2 * d)
    return pl.pallas_call(
        _branch_mix_kernel,
        grid=(n // tm, nb),
        in_specs=[
            pl.BlockSpec((tm, ka), lambda i, j: (i, 0)),
            pl.BlockSpec((tm, kb), lambda i, j: (i, 0)),
            pl.BlockSpec((ka, tn), lambda i, j: (0, j)),
            pl.BlockSpec((kb, tn), lambda i, j: (0, j)),
            pl.BlockSpec((tm, tn), lambda i, j: (i, j)),
            pl.BlockSpec((tm, tn), lambda i, j: (i, j + nb)),
            pl.BlockSpec((1, tn), lambda i, j: (0, j)),
            pl.BlockSpec((1, tn), lambda i, j: (0, j + nb)),
        ],
        out_specs=pl.BlockSpec((tm, tn), lambda i, j: (i, j)),
        out_shape=jax.ShapeDtypeStruct((n, d), BF16),
        compiler_params=_params(("parallel", "arbitrary"), vmem),
        name="branch_mix",
    )(ya, yb, wa, wb, gates, gates, bias2, bias2)


def _split_dot(x, m_bf16):
    hi = x.astype(BF16)
    lo = (x - hi.astype(F32)).astype(BF16)
    return jnp.dot(jnp.concatenate([hi, lo], axis=1), m_bf16, preferred_element_type=F32)


def _cumsum_rows(tri, x):
    return jnp.dot(tri, x, preferred_element_type=F32, precision=lax.Precision.HIGHEST)


def _softplus(x):
    return jnp.maximum(x, 0.0) + jnp.log1p(jnp.exp(-jnp.abs(x)))


def _log_sigmoid(x):
    return -_softplus(-x)


def _silu(x):
    return x * jax.nn.sigmoid(x)


def _ssd_kernel(xs_ref, xsp_ref, bm_ref, bmp_ref, cm_ref, cmp_ref, z_ref, dt_ref,
                cwx_ref, cwb_ref, cwc_ref, cbx_ref, cbb_ref, cbc_ref,
                dtb_ref, alog_ref, dskip_ref, nw_ref, expand_ref,
                o_ref, state_ref, *, chunk, headdim, n_taps, group_heads):
    c = pl.program_id(2)
    L = chunk
    first = c == 0

    @pl.when(first)
    def _():
        state_ref[...] = jnp.zeros_like(state_ref)

    row = lax.broadcasted_iota(jnp.int32, (L, 2 * L), 0)
    col = lax.broadcasted_iota(jnp.int32, (L, 2 * L), 1)
    keep_prev = jnp.logical_or(col >= L, jnp.logical_not(first))

    def shift_mat(k):
        hit = jnp.logical_and(col == row + (L - (n_taps - 1 - k)), keep_prev)
        return jnp.where(hit, 1.0, 0.0).astype(BF16)

    shifts = [shift_mat(k) for k in range(n_taps - 1)]

    def conv(cur_ref, prev_ref, w_ref, b_ref):
        cur = cur_ref[...]
        stack = jnp.concatenate([prev_ref[...], cur], axis=0)
        w = w_ref[...]
        acc = cur.astype(F32) * w[n_taps - 1:n_taps, :] + b_ref[...]
        for k in range(n_taps - 1):
            acc = acc + jnp.dot(shifts[k], stack, preferred_element_type=F32) * w[k:k + 1, :]
        return _silu(acc)

    xs = conv(xs_ref, xsp_ref, cwx_ref, cbx_ref)
    bmat = conv(bm_ref, bmp_ref, cwb_ref, cbb_ref)
    cmat = conv(cm_ref, cmp_ref, cwc_ref, cbc_ref)

    lane0 = (pl.program_id(1) * group_heads) % V7X_LANES
    dt_pre = pltpu.roll(dt_ref[...], (V7X_LANES - lane0) % V7X_LANES, 1)
    dt = _softplus(dt_pre + dtb_ref[...])
    la = dt * (-jnp.exp(alog_ref[...]))
    r2 = lax.broadcasted_iota(jnp.int32, (L, L), 0)
    c2 = lax.broadcasted_iota(jnp.int32, (L, L), 1)
    causal = c2 <= r2
    tri = jnp.where(causal, 1.0, 0.0).astype(F32)
    acum = _cumsum_rows(tri, la)
    acum_t = acum.T
    a_last = acum[L - 1:L, :]

    expand = expand_ref[...]
    dt_e = _split_dot(dt, expand)
    eac_e = _split_dot(jnp.exp(acum), expand)
    wend_e = _split_dot(jnp.exp(a_last - acum), expand)

    xdt = xs * dt_e
    xdt_b = xdt.astype(BF16)
    cb = cmat.astype(BF16)
    bb = bmat.astype(BF16)
    scores = lax.dot_general(cb, bb, (((1,), (1,)), ((), ())), preferred_element_type=F32)

    gw = xs.shape[1]
    heads_per_tile = V7X_LANES // headdim
    lane = lax.broadcasted_iota(jnp.int32, (L, V7X_LANES), 1)
    y_tiles = []
    for q in range(gw // V7X_LANES):
        ms = []
        rhs = []
        x_tile = xdt_b[:, q * V7X_LANES:(q + 1) * V7X_LANES]
        for e in range(heads_per_tile):
            j = q * heads_per_tile + e
            seg = acum[:, j:j + 1] - acum_t[j:j + 1, :]
            ms.append((scores * jnp.where(causal, jnp.exp(seg), 0.0)).astype(BF16))
            in_head = jnp.logical_and(lane >= e * headdim, lane < (e + 1) * headdim)
            rhs.append(jnp.where(in_head, x_tile, jnp.zeros_like(x_tile)))
        y_tiles.append(jnp.dot(jnp.concatenate(ms, axis=1), jnp.concatenate(rhs, axis=0),
                               preferred_element_type=F32))
    y = jnp.concatenate(y_tiles, axis=1)

    state = state_ref[...]
    y = y + jnp.dot(cb, state.astype(BF16), preferred_element_type=F32) * eac_e
    y = y + xs * dskip_ref[...]
    y = y * _silu(z_ref[...].astype(F32))
    y = y * lax.rsqrt(jnp.mean(y * y, axis=-1, keepdims=True) + RMS_EPS)
    o_ref[...] = (y * nw_ref[...]).astype(o_ref.dtype)

    upd = jnp.dot(bmat.T.astype(BF16), (wend_e * xdt).astype(BF16), preferred_element_type=F32)
    state_ref[...] = state * eac_e[L - 1:L, :] + upd


def _ssd_branch(cfg, z, xbc, dt_pad, conv_w, conv_b, dt_bias, a_log, d_skip, norm_w, bsz, t_len):
    L = cfg.ssd_chunk
    G = cfg.ssd_groups
    GW = cfg.group_width
    N = cfg.ssd_state
    J = cfg.group_heads
    nc = t_len // L
    assert V7X_LANES % cfg.ssd_headdim == 0 and J <= V7X_LANES and GW % V7X_LANES == 0 and N % V7X_LANES == 0
    nx = cfg.d_inner // GW
    b_off = cfg.d_inner // N
    c_off = (cfg.d_inner + G * N) // N

    def pad_heads(v):
        return jnp.pad(v.reshape(G, 1, J), ((0, 0), (0, 0), (0, V7X_LANES - J)))

    dskip_e = jnp.repeat(d_skip, cfg.ssd_headdim).reshape(1, cfg.d_inner)
    head_of_chan = jnp.arange(GW) // cfg.ssd_headdim
    expand = (jnp.arange(V7X_LANES)[:, None] == head_of_chan[None, :]).astype(BF16)
    expand = jnp.concatenate([expand, expand], axis=0)

    cur = lambda b, g, c: b * nc + c
    prev = lambda b, g, c: b * nc + jnp.maximum(c - 1, 0)
    kernel = functools.partial(_ssd_kernel, chunk=L, headdim=cfg.ssd_headdim, n_taps=cfg.ssd_conv, group_heads=J)
    vmem = 4 * _nbytes((L, GW), BF16) * 2 + 24 * _nbytes((L, GW), F32) + 2 * _nbytes((N, GW), F32)
    return pl.pallas_call(
        kernel,
        grid=(bsz, G, nc),
        in_specs=[
            pl.BlockSpec((L, GW), lambda b, g, c: (cur(b, g, c), g)),
            pl.BlockSpec((L, GW), lambda b, g, c: (prev(b, g, c), g)),
            pl.BlockSpec((L, N), lambda b, g, c: (cur(b, g, c), b_off + g)),
            pl.BlockSpec((L, N), lambda b, g, c: (prev(b, g, c), b_off + g)),
            pl.BlockSpec((L, N), lambda b, g, c: (cur(b, g, c), c_off + g)),
            pl.BlockSpec((L, N), lambda b, g, c: (prev(b, g, c), c_off + g)),
            pl.BlockSpec((L, GW), lambda b, g, c: (cur(b, g, c), g)),
            pl.BlockSpec((L, V7X_LANES), lambda b, g, c: (cur(b, g, c), (g * J) // V7X_LANES)),
            pl.BlockSpec((cfg.ssd_conv, GW), lambda b, g, c: (0, g)),
            pl.BlockSpec((cfg.ssd_conv, N), lambda b, g, c: (0, b_off + g)),
            pl.BlockSpec((cfg.ssd_conv, N), lambda b, g, c: (0, c_off + g)),
            pl.BlockSpec((1, GW), lambda b, g, c: (0, g)),
            pl.BlockSpec((1, N), lambda b, g, c: (0, b_off + g)),
            pl.BlockSpec((1, N), lambda b, g, c: (0, c_off + g)),
            pl.BlockSpec((None, 1, V7X_LANES), lambda b, g, c: (g, 0, 0)),
            pl.BlockSpec((None, 1, V7X_LANES), lambda b, g, c: (g, 0, 0)),
            pl.BlockSpec((1, GW), lambda b, g, c: (0, g)),
            pl.BlockSpec((1, GW), lambda b, g, c: (0, g)),
            pl.BlockSpec((2 * V7X_LANES, GW), lambda b, g, c: (0, 0)),
        ],
        out_specs=pl.BlockSpec((L, GW), lambda b, g, c: (cur(b, g, c), g)),
        out_shape=jax.ShapeDtypeStruct((bsz * t_len, cfg.d_inner), BF16),
        scratch_shapes=[pltpu.VMEM((N, GW), F32)],
        compiler_params=_params(("parallel", "parallel", "arbitrary"), vmem),
        name="ssd_scan",
    )(xbc, xbc, xbc, xbc, xbc, xbc, z, dt_pad,
      conv_w, conv_w, conv_w, conv_b.reshape(1, -1), conv_b.reshape(1, -1), conv_b.reshape(1, -1),
      pad_heads(dt_bias), pad_heads(a_log), dskip_e, norm_w.reshape(1, -1), expand)


def _mlstm_kernel(ib_ref, fb_ref, q_ref, k_ref, v_ref, o_ref, gc_ref, gr_ref, nw_ref,
                  out_ref, c_ref, n_ref, m_ref, *, chunk, n_heads, dk):
    h = pl.program_id(1)
    c = pl.program_id(2)
    L = chunk

    @pl.when(c == 0)
    def _():
        c_ref[...] = jnp.zeros_like(c_ref)
        n_ref[...] = jnp.zeros_like(n_ref)
        m_ref[...] = jnp.zeros_like(m_ref)

    ib = ib_ref[h]
    fb = fb_ref[h]
    gc = pltpu.roll(gc_ref[...], (V7X_LANES - h) % V7X_LANES, 1)
    li_c = gc[:, 0:1] + ib
    lf_c = _log_sigmoid(gc[:, n_heads:n_heads + 1] + fb)
    gr = gr_ref[...]
    li_r = gr_ref[pl.ds(h, 1), :] + ib
    lf_r = _log_sigmoid(gr_ref[pl.ds(n_heads + h, 1), :] + fb)
    del gr

    r2 = lax.broadcasted_iota(jnp.int32, (L, L), 0)
    c2 = lax.broadcasted_iota(jnp.int32, (L, L), 1)
    causal = c2 <= r2
    tri = jnp.where(causal, 1.0, 0.0).astype(F32)
    tri_t = jnp.where(r2 <= c2, 1.0, 0.0).astype(F32)
    bcum_c = _cumsum_rows(tri, jnp.broadcast_to(lf_c, (L, V7X_LANES)))[:, 0:1]
    bcum_r = jnp.dot(jnp.broadcast_to(lf_r, (V7X_SUBLANES, L)), tri_t, preferred_element_type=F32,
                     precision=lax.Precision.HIGHEST)[0:1, :]
    b_last = bcum_c[L - 1:L, :]

    m_prev = m_ref[...]
    dlog = jnp.where(causal, bcum_c - bcum_r + li_r, -jnp.inf)
    inter_log = bcum_c + m_prev
    m_t = jnp.maximum(inter_log, jnp.max(dlog, axis=1, keepdims=True))
    w_intra = jnp.exp(dlog - m_t)
    w_inter = jnp.exp(inter_log - m_t)

    q = q_ref[...]
    k = k_ref[...]
    v = v_ref[...]
    scale = dk ** -0.5
    s = lax.dot_general(q, k, (((1,), (1,)), ((), ())), preferred_element_type=F32) * scale * w_intra
    c_st = c_ref[...]
    n_st = n_ref[...]
    qf = q.astype(F32) * scale
    num = jnp.dot(s.astype(BF16), v, preferred_element_type=F32)
    num = num + w_inter * (jnp.dot(q, c_st.astype(BF16), preferred_element_type=F32) * scale)
    den = jnp.sum(s, axis=1, keepdims=True) + w_inter * jnp.sum(qf * n_st, axis=1, keepdims=True)
    hid = num / jnp.maximum(jnp.abs(den), jnp.exp(-m_t))
    hid = hid * lax.rsqrt(jnp.mean(hid * hid, axis=-1, keepdims=True) + RMS_EPS)
    out_ref[...] = (hid * nw_ref[...] * jax.nn.sigmoid(o_ref[...].astype(F32))).astype(out_ref.dtype)

    end_log = b_last - bcum_c + li_c
    m_new = jnp.maximum(b_last + m_prev, jnp.max(end_log, axis=0, keepdims=True))
    w_end = jnp.exp(end_log - m_new)
    w_carry = jnp.exp(b_last + m_prev - m_new)
    kw = k.astype(F32) * w_end
    c_ref[...] = w_carry * c_st + jnp.dot(kw.T.astype(BF16), v, preferred_element_type=F32)
    n_ref[...] = w_carry * n_st + jnp.sum(kw, axis=0, keepdims=True)
    m_ref[...] = m_new


def _mlstm_branch(cfg, qk, v, o_pre, gate_cols, gate_rows, igate_b, fgate_b, norm_w, bsz, t_len):
    L = cfg.ml_chunk
    H = cfg.ml_heads
    dk, dv = cfg.ml_dk, cfg.ml_dv
    nc = t_len // L
    assert dk % V7X_LANES == 0 and dv % V7X_LANES == 0
    row = lambda b, h, c: b * nc + c
    kernel = functools.partial(_mlstm_kernel, chunk=L, n_heads=H, dk=dk)
    vmem = 4 * (2 * _nbytes((L, dk), BF16) + 3 * _nbytes((L, dv), BF16)) + 3 * _nbytes((dk, dv), F32) \
        + 10 * _nbytes((L, L), F32) + 8 * _nbytes((L, dv), F32)
    smem = pl.BlockSpec(memory_space=pltpu.SMEM)
    return pl.pallas_call(
        kernel,
        grid=(bsz, H, nc),
        in_specs=[
            smem, smem,
            pl.BlockSpec((L, dk), lambda b, h, c: (row(b, h, c), h)),
            pl.BlockSpec((L, dk), lambda b, h, c: (row(b, h, c), H + h)),
            pl.BlockSpec((L, dv), lambda b, h, c: (row(b, h, c), h)),
            pl.BlockSpec((L, dv), lambda b, h, c: (row(b, h, c), h)),
            pl.BlockSpec((L, V7X_LANES), lambda b, h, c: (row(b, h, c), 0)),
            pl.BlockSpec((2 * H, L), lambda b, h, c: (0, row(b, h, c))),
            pl.BlockSpec((1, dv), lambda b, h, c: (0, h)),
        ],
        out_specs=pl.BlockSpec((L, dv), lambda b, h, c: (row(b, h, c), h)),
        out_shape=jax.ShapeDtypeStruct((bsz * t_len, cfg.ml_v), BF16),
        scratch_shapes=[pltpu.VMEM((dk, dv), F32), pltpu.VMEM((1, dk), F32), pltpu.VMEM((1, 1), F32)],
        compiler_params=_params(("parallel", "parallel", "arbitrary"), vmem),
        name="mlstm_scan",
    )(igate_b, fgate_b, qk, qk, v, o_pre, gate_cols, gate_rows, norm_w.reshape(1, -1))


def _peer_scores_kernel(q_ref, k1_ref, k2_ref, s_ref, *, half):
    q = q_ref[...]
    nt = (((1,), (1,)), ((), ()))
    s_ref[0] = lax.dot_general(k1_ref[...], q[:, :half], nt, preferred_element_type=F32)
    s_ref[1] = lax.dot_general(k2_ref[...], q[:, half:], nt, preferred_element_type=F32)


def _peer_scores(cfg, q, k1, k2):
    n = q.shape[0]
    H, nk, half = cfg.peer_heads, cfg.peer_nkeys, cfg.peer_half
    tm = _pick(n, (1024, 512, 256, 128))
    vmem = 4 * _nbytes((tm, 2 * half), BF16) + 4 * _nbytes((2, nk, tm), F32)
    return pl.pallas_call(
        functools.partial(_peer_scores_kernel, half=half),
        grid=(n // tm, H),
        in_specs=[
            pl.BlockSpec((tm, 2 * half), lambda i, h: (i, h)),
            pl.BlockSpec((None, nk, half), lambda i, h: (h, 0, 0)),
            pl.BlockSpec((None, nk, half), lambda i, h: (h, 0, 0)),
        ],
        out_specs=pl.BlockSpec((None, 2, nk, tm), lambda i, h: (h, 0, 0, i)),
        out_shape=jax.ShapeDtypeStruct((H, 2, nk, n), F32),
        compiler_params=_params(("parallel", "arbitrary"), vmem),
        name="peer_scores",
    )(q, k1, k2)


def _candidate_pairs(topk):
    return [(i, j) for i in range(topk) for j in range(topk) if (i + 1) * (j + 1) <= topk]


def _peer_stats_kernel(s_ref, tau_ref, mu_ref, *, nkeys, topk):
    shape = tau_ref.shape
    neg = jnp.full(shape, -jnp.inf, F32)

    def insert(lst, x):
        out = []
        for cur in lst:
            out.append(jnp.maximum(cur, x))
            x = jnp.minimum(cur, x)
        return tuple(out)

    def top_of_half(r):
        def body(kk, lst):
            return insert(lst, s_ref[r, kk])
        return lax.fori_loop(0, nkeys, body, tuple(neg for _ in range(topk)), unroll=STATS_UNROLL)

    a = top_of_half(0)
    b = top_of_half(1)
    cands = [a[i] + b[j] for (i, j) in _candidate_pairs(topk)]
    best = tuple(neg for _ in range(topk))
    for cnd in cands:
        best = insert(best, cnd)
    tau = best[topk - 1]
    m = best[0]
    z = jnp.zeros(shape, F32)
    for cnd in cands:
        z = z + jnp.where(cnd >= tau, jnp.exp(cnd - m), 0.0)
    tau_ref[...] = tau
    mu_ref[...] = m + jnp.log(z)


def _peer_stats(cfg, s):
    H, _, nk, n = s.shape
    rows = n // V7X_LANES
    tr = _pick(rows, (2 * V7X_SUBLANES, V7X_SUBLANES))
    s5 = s.reshape(H, 2, nk, rows, V7X_LANES)
    vmem = 4 * _nbytes((2, nk, tr, V7X_LANES), F32)
    out = jax.ShapeDtypeStruct((H, rows, V7X_LANES), F32)
    tau, mu = pl.pallas_call(
        functools.partial(_peer_stats_kernel, nkeys=nk, topk=cfg.peer_topk),
        grid=(H, rows // tr),
        in_specs=[pl.BlockSpec((None, 2, nk, tr, V7X_LANES), lambda h, i: (h, 0, 0, i, 0))],
        out_specs=[pl.BlockSpec((None, tr, V7X_LANES), lambda h, i: (h, i, 0))] * 2,
        out_shape=[out, out],
        compiler_params=_params(("parallel", "parallel"), vmem),
        name="peer_stats",
    )(s5)
    return tau.reshape(H, n), mu.reshape(H, n)


def _peer_dense_kernel(xt_ref, u_ref, v_ref, s1_ref, s2_ref, tau_ref, mu_ref, yt_ref, st_scr, w_scr,
                       *, n_heads, sub, nkeys, pieces, d_chunk):
    j = pl.program_id(1)

    @pl.when(j == 0)
    def _():
        yt_ref[...] = jnp.zeros_like(yt_ref)

    rows_p = sub * nkeys // pieces
    sub_p = sub // pieces
    tn = (((0,), (0,)), ((), ()))
    xt = xt_ref[...]
    for p in range(pieces):
        st_scr[p] = jnp.dot(u_ref[p * rows_p:(p + 1) * rows_p, :], xt, preferred_element_type=F32)
    for p in range(pieces):
        for a in range(sub_p):
            i1 = j * sub + p * sub_p + a
            w = None
            for h in range(n_heads):
                cnd = s1_ref[h, pl.ds(i1, 1), :] + s2_ref[h]
                wh = jnp.where(cnd >= tau_ref[pl.ds(h, 1), :], jnp.exp(cnd - mu_ref[pl.ds(h, 1), :]), 0.0)
                w = wh if w is None else w + wh
            st = st_scr[p, a * nkeys:(a + 1) * nkeys, :]
            act = 0.5 * st * (1.0 + lax.erf(st * (2.0 ** -0.5)))
            w_scr[p, a * nkeys:(a + 1) * nkeys, :] = (w * act).astype(BF16)
        for dc in range(yt_ref.shape[0] // d_chunk):
            rows = pl.ds(dc * d_chunk, d_chunk)
            yt_ref[rows, :] += lax.dot_general(v_ref[p * rows_p:(p + 1) * rows_p, rows], w_scr[p], tn,
                                               preferred_element_type=F32)


def _peer_dense(cfg, xt_bf16, u_bf16, vt_bf16, s, tau, mu):
    d, n = xt_bf16.shape
    H, nk = cfg.peer_heads, cfg.peer_nkeys
    E = cfg.peer_experts
    tm = _pick(n, (512, 256, 128))
    te = _pick(E, (512, 256, 128))
    d_chunk = _pick(d, (512, 256, 128))
    sub = te // nk
    pieces = 2 if sub % 2 == 0 else 1
    assert nk == V7X_LANES and te % nk == 0
    once = pl.Buffered(1)
    vmem = (_nbytes((tm, d), BF16) + 4 * _nbytes((te, d), BF16) + 2 * _nbytes((H, nk, tm), F32)
            + 2 * _nbytes((d, tm), F32) + 12 * _nbytes((te, tm), F32) + 2 * _nbytes((d_chunk, tm), F32))
    return pl.pallas_call(
        functools.partial(_peer_dense_kernel, n_heads=H, sub=sub, nkeys=nk, pieces=pieces, d_chunk=d_chunk),
        grid=(n // tm, E // te),
        in_specs=[
            pl.BlockSpec((d, tm), lambda i, j: (0, i), pipeline_mode=once),
            pl.BlockSpec((te, d), lambda i, j: (j, 0)),
            pl.BlockSpec((te, d), lambda i, j: (j, 0)),
            pl.BlockSpec((H, None, nk, tm), lambda i, j: (0, 0, 0, i), pipeline_mode=once),
            pl.BlockSpec((H, None, nk, tm), lambda i, j: (0, 1, 0, i), pipeline_mode=once),
            pl.BlockSpec((H, tm), lambda i, j: (0, i), pipeline_mode=once),
            pl.BlockSpec((H, tm), lambda i, j: (0, i), pipeline_mode=once),
        ],
        out_specs=pl.BlockSpec((d, tm), lambda i, j: (0, i)),
        out_shape=jax.ShapeDtypeStruct((d, n), F32),
        scratch_shapes=[pltpu.VMEM((pieces, te // pieces, tm), F32), pltpu.VMEM((pieces, te // pieces, tm), BF16)],
        compiler_params=_params(("parallel", "arbitrary"), vmem),
        name="peer_dense",
    )(xt_bf16, u_bf16, vt_bf16, s, s, tau, mu)


def _mixer(cfg, h_bf16, bsz, t_len, w_in, conv_w, conv_b, dt_bias, a_log, d_skip, ssm_norm_w, w_branch_a,
           igate_b, fgate_b, mlstm_norm_w, w_branch_b, gate_b, w_out):
    d = cfg.d_model
    G, J, H = cfg.ssd_groups, cfg.group_heads, cfg.ml_heads
    splits = (cfg.d_inner, cfg.conv_dim, cfg.ssd_heads, cfg.ml_qk, cfg.ml_qk, cfg.ml_v, cfg.ml_v, H, H, d, d)
    offs = [0]
    for sp in splits:
        offs.append(offs[-1] + sp)
    seg = lambda a, b: w_in[:, offs[a]:offs[b]]
    w_z = seg(0, 1).astype(BF16)
    w_xbc = seg(1, 2).astype(BF16)
    def pad_lanes(w):
        return jnp.pad(w, ((0, 0), (0, -w.shape[1] % V7X_LANES)))

    w_dt = pad_lanes(seg(2, 3)).astype(BF16)
    w_qk = seg(3, 5).astype(BF16)
    w_v = seg(5, 6).astype(BF16)
    w_o = seg(6, 7).astype(BF16)
    w_i, w_f = seg(7, 8), seg(8, 9)
    w_gc = pad_lanes(jnp.concatenate([w_i, w_f], axis=1)).astype(BF16)
    w_gr = jnp.concatenate([w_i, w_f], axis=1).T.astype(BF16)
    w_g = seg(9, 11).astype(BF16)

    z = _matmul(h_bf16, w_z, BF16, "proj_z")
    xbc = _matmul(h_bf16, w_xbc, BF16, "proj_xbc")
    dt_pad = _matmul(h_bf16, w_dt, F32, "proj_dt")
    qk = _matmul(h_bf16, w_qk, BF16, "proj_qk")
    v = _matmul(h_bf16, w_v, BF16, "proj_v")
    o_pre = _matmul(h_bf16, w_o, BF16, "proj_o")
    gate_cols = _matmul(h_bf16, w_gc, F32, "proj_gate_cols")
    gate_rows = _matmul_nt(w_gr, h_bf16, F32, "proj_gate_rows")
    gates = _matmul(h_bf16, w_g, BF16, "proj_gates")

    y_a = _ssd_branch(cfg, z, xbc, dt_pad, conv_w, conv_b, dt_bias, a_log, d_skip, ssm_norm_w, bsz, t_len)
    y_b = _mlstm_branch(cfg, qk, v, o_pre, gate_cols, gate_rows, igate_b, fgate_b, mlstm_norm_w, bsz, t_len)
    mixed = _branch_mix(y_a, y_b, w_branch_a.astype(BF16), w_branch_b.astype(BF16), gates, gate_b, d)
    return _matmul(mixed, w_out.astype(BF16), F32, "proj_out")


def _peer(cfg, h_bf16, ht_bf16, wq, k1, k2, u_tab, v_tab):
    q = _matmul(h_bf16, wq.astype(BF16), BF16, "peer_q")
    s = _peer_scores(cfg, q, k1.astype(BF16), k2.astype(BF16))
    tau, mu = _peer_stats(cfg, s)
    return _peer_dense(cfg, ht_bf16, u_tab.astype(BF16), v_tab.astype(BF16), s, tau, mu)


def _forward(cfg, x, emb_ln_g, emb_ln_b, w_in, conv_w, conv_b, dt_bias, a_log, d_skip, ssm_norm_w,
             w_branch_a, igate_b, fgate_b, mlstm_norm_w, w_branch_b, gate_b, w_out, ln1_g, ln1_b,
             peer_wq, peer_k1, peer_k2, peer_u, peer_v, ln2_g, ln2_b):
    bsz, t_len, d = x.shape
    n = bsz * t_len
    alpha = (2 * cfg.depth) ** 0.25
    tm = _pick(n, (256, 128))
    h, hb = _ln_call(_ln_kernel, [x.reshape(n, d)], emb_ln_g, emb_ln_b, n, d, tm)
    for l in range(cfg.depth):
        mix = _mixer(cfg, hb, bsz, t_len, w_in[l], conv_w[l], conv_b[l], dt_bias[l], a_log[l], d_skip[l],
                     ssm_norm_w[l], w_branch_a[l], igate_b[l], fgate_b[l], mlstm_norm_w[l], w_branch_b[l],
                     gate_b[l], w_out[l])
        h, hb, hbt = _ln_call(functools.partial(_res_ln_kernel, alpha=alpha), [h, mix], ln1_g[l], ln1_b[l], n, d, tm,
                              emit_transposed=True)
        ffn_t = _peer(cfg, hb, hbt, peer_wq[l], peer_k1[l], peer_k2[l], peer_u[l], peer_v[l])
        h, hb = _ln_call(functools.partial(_res_ln_t_kernel, alpha=alpha), [h, ffn_t], ln2_g[l], ln2_b[l], n, d, tm,
                         transposed_last=True)
    return h.reshape(bsz, t_len, d)


def kernel(x, emb_ln_g, emb_ln_b, w_in, conv_w, conv_b, dt_bias, a_log, d_skip, ssm_norm_w, w_branch_a, igate_b,
           fgate_b, mlstm_norm_w, w_branch_b, gate_b, w_out, ln1_g, ln1_b, peer_wq, peer_k1, peer_k2, peer_u,
           peer_v, ln2_g, ln2_b):
    return _forward(Cfg(), x, emb_ln_g, emb_ln_b, w_in, conv_w, conv_b, dt_bias, a_log, d_skip, ssm_norm_w,
                    w_branch_a, igate_b, fgate_b, mlstm_norm_w, w_branch_b, gate_b, w_out, ln1_g, ln1_b,
                    peer_wq, peer_k1, peer_k2, peer_u, peer_v, ln2_g, ln2_b)
```

```python
import functools
import math
from typing import NamedTuple

import jax
import jax.numpy as jnp
from jax import lax
from jax.experimental import pallas as pl
from jax.experimental.pallas import tpu as pltpu

F32 = jnp.float32
BF16 = jnp.bfloat16

V7X_LANES = 128
V7X_SUBLANES = 8
V7X_VMEM_BYTES = 64 * 1024 * 1024
VMEM_BUDGET_BYTES = 56 * 1024 * 1024

STATS_UNROLL = 4

LN_EPS = 1e-5
RMS_EPS = 1e-6


class Cfg(NamedTuple):
    depth: int = 2
    d_model: int = 4096
    d_inner: int = 8192
    ssd_headdim: int = 64
    ssd_groups: int = 8
    ssd_state: int = 128
    ssd_conv: int = 4
    ssd_chunk: int = 128
    ml_heads: int = 8
    ml_qk: int = 2048
    ml_v: int = 4096
    ml_chunk: int = 256
    peer_heads: int = 8
    peer_nkeys: int = 128
    peer_dkey: int = 256
    peer_topk: int = 16

    @property
    def ssd_heads(self):
        return self.d_inner // self.ssd_headdim

    @property
    def group_heads(self):
        return self.ssd_heads // self.ssd_groups

    @property
    def group_width(self):
        return self.d_inner // self.ssd_groups

    @property
    def conv_dim(self):
        return self.d_inner + 2 * self.ssd_groups * self.ssd_state

    @property
    def ml_dk(self):
        return self.ml_qk // self.ml_heads

    @property
    def ml_dv(self):
        return self.ml_v // self.ml_heads

    @property
    def peer_half(self):
        return self.peer_dkey // 2

    @property
    def peer_experts(self):
        return self.peer_nkeys * self.peer_nkeys


def _params(semantics, vmem_bytes, flags=None):
    return pltpu.CompilerParams(dimension_semantics=semantics, flags=flags,
                                vmem_limit_bytes=int(min(max(vmem_bytes, 16 * 1024 * 1024), VMEM_BUDGET_BYTES)))


def _nbytes(shape, dtype):
    return math.prod(shape) * jnp.dtype(dtype).itemsize


def _pick(n, prefs):
    for p in prefs:
        if n % p == 0:
            return p
    return n


def _ln_rows(x, g, b):
    mu = jnp.mean(x, axis=-1, keepdims=True)
    xc = x - mu
    var = jnp.mean(xc * xc, axis=-1, keepdims=True)
    return xc * lax.rsqrt(var + LN_EPS) * g + b


def _ln_kernel(x_ref, g_ref, b_ref, of_ref, ob_ref):
    y = _ln_rows(x_ref[...], g_ref[...], b_ref[...])
    of_ref[...] = y
    ob_ref[...] = y.astype(BF16)


def _res_ln_kernel(h_ref, f_ref, g_ref, b_ref, of_ref, ob_ref, obt_ref, *, alpha):
    y = _ln_rows(alpha * h_ref[...] + f_ref[...], g_ref[...], b_ref[...])
    of_ref[...] = y
    ob_ref[...] = y.astype(BF16)
    obt_ref[...] = y.T.astype(BF16)


def _res_ln_t_kernel(h_ref, ft_ref, g_ref, b_ref, of_ref, ob_ref, *, alpha):
    y = _ln_rows(alpha * h_ref[...] + ft_ref[...].T, g_ref[...], b_ref[...])
    of_ref[...] = y
    ob_ref[...] = y.astype(BF16)


def _ln_call(kernel_fn, row_inputs, g, b, n, d, tm, transposed_last=False, emit_transposed=False):
    row_spec = pl.BlockSpec((tm, d), lambda i: (i, 0))
    col_spec = pl.BlockSpec((d, tm), lambda i: (0, i))
    vec_spec = pl.BlockSpec((1, d), lambda i: (0, 0))
    in_specs = [col_spec if (transposed_last and k == len(row_inputs) - 1) else row_spec
                for k in range(len(row_inputs))]
    out_specs = [row_spec, row_spec]
    out_shape = [jax.ShapeDtypeStruct((n, d), F32), jax.ShapeDtypeStruct((n, d), BF16)]
    if emit_transposed:
        out_specs.append(col_spec)
        out_shape.append(jax.ShapeDtypeStruct((d, n), BF16))
    vmem = 2 * (len(row_inputs) + 1) * _nbytes((tm, d), F32) + 4 * _nbytes((tm, d), BF16) + 5 * _nbytes((tm, d), F32)
    return pl.pallas_call(
        kernel_fn,
        grid=(n // tm,),
        in_specs=in_specs + [vec_spec, vec_spec],
        out_specs=out_specs,
        out_shape=out_shape,
        compiler_params=_params(("parallel",), vmem),
    )(*row_inputs, g.reshape(1, d), b.reshape(1, d))


def _mm_kernel(a_ref, b_ref, o_ref):
    o_ref[...] = jnp.dot(a_ref[...], b_ref[...], preferred_element_type=F32).astype(o_ref.dtype)


def _matmul(a, b, out_dtype, name):
    m, k = a.shape
    _, n = b.shape
    tn = _pick(n, (512, 256, 128))
    tm = _pick(m, (1024, 512, 256, 128)) if k <= 4096 else _pick(m, (512, 256, 128))
    vmem = 2 * (_nbytes((tm, k), BF16) + _nbytes((k, tn), BF16) + _nbytes((tm, tn), out_dtype)) + 2 * _nbytes((tm, tn), F32)
    return pl.pallas_call(
        _mm_kernel,
        grid=(m // tm, n // tn),
        in_specs=[pl.BlockSpec((tm, k), lambda i, j: (i, 0)), pl.BlockSpec((k, tn), lambda i, j: (0, j))],
        out_specs=pl.BlockSpec((tm, tn), lambda i, j: (i, j)),
        out_shape=jax.ShapeDtypeStruct((m, n), out_dtype),
        compiler_params=_params(("parallel", "arbitrary"), vmem),
        name=name,
    )(a, b)


def _mm_nt_kernel(w_ref, a_ref, o_ref):
    o_ref[...] = lax.dot_general(w_ref[...], a_ref[...], (((1,), (1,)), ((), ())),
                                 preferred_element_type=F32).astype(o_ref.dtype)


def _matmul_nt(w_t, a, out_dtype, name):
    f, k = w_t.shape
    m, _ = a.shape
    tm = _pick(m, (1024, 512, 256, 128))
    vmem = 2 * (_nbytes((tm, k), BF16) + _nbytes((f, k), BF16) + _nbytes((f, tm), out_dtype)) + 2 * _nbytes((f, tm), F32)
    return pl.pallas_call(
        _mm_nt_kernel,
        grid=(m // tm,),
        in_specs=[pl.BlockSpec((f, k), lambda i: (0, 0)), pl.BlockSpec((tm, k), lambda i: (i, 0))],
        out_specs=pl.BlockSpec((f, tm), lambda i: (0, i)),
        out_shape=jax.ShapeDtypeStruct((f, m), out_dtype),
        compiler_params=_params(("parallel",), vmem),
        name=name,
    )(w_t, a)


def _branch_mix_kernel(ya_ref, yb_ref, wa_ref, wb_ref, ga_ref, gb_ref, ba_ref, bb_ref, o_ref):
    acc_a = jnp.dot(ya_ref[...], wa_ref[...], preferred_element_type=F32)
    acc_b = jnp.dot(yb_ref[...], wb_ref[...], preferred_element_type=F32)
    gate_a = jax.nn.sigmoid(ga_ref[...].astype(F32) + ba_ref[...])
    gate_b = jax.nn.sigmoid(gb_ref[...].astype(F32) + bb_ref[...])
    o_ref[...] = (gate_a * acc_a + gate_b * acc_b).astype(o_ref.dtype)


def _branch_mix(ya, yb, wa, wb, gates, gate_bias, d):
    n, ka = ya.shape
    _, kb = yb.shape
    tm = _pick(n, (512, 256, 128))
    tn = _pick(d, (256, 128))
    nb = d // tn
    vmem = 2 * (_nbytes((tm, ka + kb), BF16) + _nbytes((ka + kb, tn), BF16) + 3 * _nbytes((tm, tn), BF16)) + 6 * _nbytes((tm, tn), F32)
    bias2 = gate_bias.reshape(1, 2 * d)
    return pl.pallas_call(
        _branch_mix_kernel,
        grid=(n // tm, nb),
        in_specs=[
            pl.BlockSpec((tm, ka), lambda i, j: (i, 0)),
            pl.BlockSpec((tm, kb), lambda i, j: (i, 0)),
            pl.BlockSpec((ka, tn), lambda i, j: (0, j)),
            pl.BlockSpec((kb, tn), lambda i, j: (0, j)),
            pl.BlockSpec((tm, tn), lambda i, j: (i, j)),
            pl.BlockSpec((tm, tn), lambda i, j: (i, j + nb)),
            pl.BlockSpec((1, tn), lambda i, j: (0, j)),
            pl.BlockSpec((1, tn), lambda i, j: (0, j + nb)),
        ],
        out_specs=pl.BlockSpec((tm, tn), lambda i, j: (i, j)),
        out_shape=jax.ShapeDtypeStruct((n, d), BF16),
        compiler_params=_params(("parallel", "arbitrary"), vmem),
        name="branch_mix",
    )(ya, yb, wa, wb, gates, gates, bias2, bias2)


CAST_BLOCK_BYTES = 8 * 1024 * 1024


def _cast_rows(rows, cols):
    for tr in (1024, 512, 256, 128, 64, 32, 16, 8):
        if rows % tr == 0 and tr * cols * 4 <= CAST_BLOCK_BYTES:
            return tr
    return rows


def _cast_kernel(w_ref, o_ref):
    o_ref[...] = w_ref[...].astype(BF16)


def _cast_layer(w_stack, l, name):
    _, r, c = w_stack.shape
    tr = _cast_rows(r, c)
    vmem = 3 * _nbytes((tr, c), F32) + 2 * _nbytes((tr, c), BF16)
    return pl.pallas_call(
        _cast_kernel,
        grid=(r // tr,),
        in_specs=[pl.BlockSpec((None, tr, c), lambda i: (l, i, 0))],
        out_specs=pl.BlockSpec((tr, c), lambda i: (i, 0)),
        out_shape=jax.ShapeDtypeStruct((r, c), BF16),
        compiler_params=_params(("parallel",), vmem),
        name=name,
    )(w_stack)


def _split_cast_kernel(w_ref, *o_refs, segs):
    for o_ref, (start, width, valid) in zip(o_refs, segs):
        blk = w_ref[:, start:start + width]
        if valid < width:
            lane = lax.broadcasted_iota(jnp.int32, blk.shape, 1)
            blk = jnp.where(lane < valid, blk, 0.0)
        o_ref[...] = blk.astype(BF16)


def _split_cast_layer(w_stack, l, segs, name):
    _, r, c = w_stack.shape
    tr = _cast_rows(r, c)
    total = sum(w for _, w, _ in segs)
    vmem = 3 * _nbytes((tr, c), F32) + 3 * _nbytes((tr, total), BF16)
    return pl.pallas_call(
        functools.partial(_split_cast_kernel, segs=tuple(segs)),
        grid=(r // tr,),
        in_specs=[pl.BlockSpec((None, tr, c), lambda i: (l, i, 0))],
        out_specs=[pl.BlockSpec((tr, w), lambda i: (i, 0)) for _, w, _ in segs],
        out_shape=[jax.ShapeDtypeStruct((r, w), BF16) for _, w, _ in segs],
        compiler_params=_params(("parallel",), vmem),
        name=name,
    )(w_stack)


def _split_dot(x, m_bf16):
    hi = x.astype(BF16)
    lo = (x - hi.astype(F32)).astype(BF16)
    return jnp.dot(jnp.concatenate([hi, lo], axis=1), m_bf16, preferred_element_type=F32)


def _cumsum_rows(tri, x):
    return jnp.dot(tri, x, preferred_element_type=F32, precision=lax.Precision.HIGHEST)


def _softplus(x):
    return jnp.maximum(x, 0.0) + jnp.log1p(jnp.exp(-jnp.abs(x)))


def _log_sigmoid(x):
    return -_softplus(-x)


def _silu(x):
    return x * jax.nn.sigmoid(x)


def _ssd_kernel(xs_ref, xsp_ref, bm_ref, bmp_ref, cm_ref, cmp_ref, z_ref, dt_ref,
                cwx_ref, cwb_ref, cwc_ref, cbx_ref, cbb_ref, cbc_ref,
                dtb_ref, alog_ref, dskip_ref, nw_ref, expand_ref,
                o_ref, state_ref, *, chunk, headdim, n_taps, group_heads):
    c = pl.program_id(2)
    L = chunk
    first = c == 0

    @pl.when(first)
    def _():
        state_ref[...] = jnp.zeros_like(state_ref)

    row = lax.broadcasted_iota(jnp.int32, (L, 2 * L), 0)
    col = lax.broadcasted_iota(jnp.int32, (L, 2 * L), 1)
    keep_prev = jnp.logical_or(col >= L, jnp.logical_not(first))

    def shift_mat(k):
        hit = jnp.logical_and(col == row + (L - (n_taps - 1 - k)), keep_prev)
        return jnp.where(hit, 1.0, 0.0).astype(BF16)

    shifts = [shift_mat(k) for k in range(n_taps - 1)]

    def conv(cur_ref, prev_ref, w_ref, b_ref):
        cur = cur_ref[...]
        stack = jnp.concatenate([prev_ref[...], cur], axis=0)
        w = w_ref[...]
        acc = cur.astype(F32) * w[n_taps - 1:n_taps, :] + b_ref[...]
        for k in range(n_taps - 1):
            acc = acc + jnp.dot(shifts[k], stack, preferred_element_type=F32) * w[k:k + 1, :]
        return _silu(acc)

    xs = conv(xs_ref, xsp_ref, cwx_ref, cbx_ref)
    bmat = conv(bm_ref, bmp_ref, cwb_ref, cbb_ref)
    cmat = conv(cm_ref, cmp_ref, cwc_ref, cbc_ref)

    lane0 = (pl.program_id(1) * group_heads) % V7X_LANES
    dt_pre = pltpu.roll(dt_ref[...], (V7X_LANES - lane0) % V7X_LANES, 1)
    dt = _softplus(dt_pre + dtb_ref[...])
    la = dt * (-jnp.exp(alog_ref[...]))
    r2 = lax.broadcasted_iota(jnp.int32, (L, L), 0)
    c2 = lax.broadcasted_iota(jnp.int32, (L, L), 1)
    causal = c2 <= r2
    tri = jnp.where(causal, 1.0, 0.0).astype(F32)
    acum = _cumsum_rows(tri, la)
    acum_t = acum.T
    a_last = acum[L - 1:L, :]

    expand = expand_ref[...]
    dt_e = _split_dot(dt, expand)
    eac_e = _split_dot(jnp.exp(acum), expand)
    wend_e = _split_dot(jnp.exp(a_last - acum), expand)

    xdt = xs * dt_e
    xdt_b = xdt.astype(BF16)
    cb = cmat.astype(BF16)
    bb = bmat.astype(BF16)
    scores = lax.dot_general(cb, bb, (((1,), (1,)), ((), ())), preferred_element_type=F32)

    gw = xs.shape[1]
    heads_per_tile = V7X_LANES // headdim
    lane = lax.broadcasted_iota(jnp.int32, (L, V7X_LANES), 1)
    y_tiles = []
    for q in range(gw // V7X_LANES):
        ms = []
        rhs = []
        x_tile = xdt_b[:, q * V7X_LANES:(q + 1) * V7X_LANES]
        for e in range(heads_per_tile):
            j = q * heads_per_tile + e
            seg = acum[:, j:j + 1] - acum_t[j:j + 1, :]
            ms.append((scores * jnp.where(causal, jnp.exp(seg), 0.0)).astype(BF16))
            in_head = jnp.logical_and(lane >= e * headdim, lane < (e + 1) * headdim)
            rhs.append(jnp.where(in_head, x_tile, jnp.zeros_like(x_tile)))
        y_tiles.append(jnp.dot(jnp.concatenate(ms, axis=1), jnp.concatenate(rhs, axis=0),
                               preferred_element_type=F32))
    y = jnp.concatenate(y_tiles, axis=1)

    state = state_ref[...]
    y = y + jnp.dot(cb, state.astype(BF16), preferred_element_type=F32) * eac_e
    y = y + xs * dskip_ref[...]
    y = y * _silu(z_ref[...].astype(F32))
    y = y * lax.rsqrt(jnp.mean(y * y, axis=-1, keepdims=True) + RMS_EPS)
    o_ref[...] = (y * nw_ref[...]).astype(o_ref.dtype)

    upd = jnp.dot(bmat.T.astype(BF16), (wend_e * xdt).astype(BF16), preferred_element_type=F32)
    state_ref[...] = state * eac_e[L - 1:L, :] + upd


def _ssd_branch(cfg, z, xbc, dt_pad, conv_w, conv_b, dt_bias, a_log, d_skip, norm_w, bsz, t_len):
    L = cfg.ssd_chunk
    G = cfg.ssd_groups
    GW = cfg.group_width
    N = cfg.ssd_state
    J = cfg.group_heads
    nc = t_len // L
    assert V7X_LANES % cfg.ssd_headdim == 0 and J <= V7X_LANES and GW % V7X_LANES == 0 and N % V7X_LANES == 0
    nx = cfg.d_inner // GW
    b_off = cfg.d_inner // N
    c_off = (cfg.d_inner + G * N) // N

    def pad_heads(v):
        return jnp.pad(v.reshape(G, 1, J), ((0, 0), (0, 0), (0, V7X_LANES - J)))

    dskip_e = jnp.repeat(d_skip, cfg.ssd_headdim).reshape(1, cfg.d_inner)
    head_of_chan = jnp.arange(GW) // cfg.ssd_headdim
    expand = (jnp.arange(V7X_LANES)[:, None] == head_of_chan[None, :]).astype(BF16)
    expand = jnp.concatenate([expand, expand], axis=0)

    cur = lambda b, g, c: b * nc + c
    prev = lambda b, g, c: b * nc + jnp.maximum(c - 1, 0)
    kernel = functools.partial(_ssd_kernel, chunk=L, headdim=cfg.ssd_headdim, n_taps=cfg.ssd_conv, group_heads=J)
    vmem = 4 * _nbytes((L, GW), BF16) * 2 + 24 * _nbytes((L, GW), F32) + 2 * _nbytes((N, GW), F32)
    return pl.pallas_call(
        kernel,
        grid=(bsz, G, nc),
        in_specs=[
            pl.BlockSpec((L, GW), lambda b, g, c: (cur(b, g, c), g)),
            pl.BlockSpec((L, GW), lambda b, g, c: (prev(b, g, c), g)),
            pl.BlockSpec((L, N), lambda b, g, c: (cur(b, g, c), b_off + g)),
            pl.BlockSpec((L, N), lambda b, g, c: (prev(b, g, c), b_off + g)),
            pl.BlockSpec((L, N), lambda b, g, c: (cur(b, g, c), c_off + g)),
            pl.BlockSpec((L, N), lambda b, g, c: (prev(b, g, c), c_off + g)),
            pl.BlockSpec((L, GW), lambda b, g, c: (cur(b, g, c), g)),
            pl.BlockSpec((L, V7X_LANES), lambda b, g, c: (cur(b, g, c), (g * J) // V7X_LANES)),
            pl.BlockSpec((cfg.ssd_conv, GW), lambda b, g, c: (0, g)),
            pl.BlockSpec((cfg.ssd_conv, N), lambda b, g, c: (0, b_off + g)),
            pl.BlockSpec((cfg.ssd_conv, N), lambda b, g, c: (0, c_off + g)),
            pl.BlockSpec((1, GW), lambda b, g, c: (0, g)),
            pl.BlockSpec((1, N), lambda b, g, c: (0, b_off + g)),
            pl.BlockSpec((1, N), lambda b, g, c: (0, c_off + g)),
            pl.BlockSpec((None, 1, V7X_LANES), lambda b, g, c: (g, 0, 0)),
            pl.BlockSpec((None, 1, V7X_LANES), lambda b, g, c: (g, 0, 0)),
            pl.BlockSpec((1, GW), lambda b, g, c: (0, g)),
            pl.BlockSpec((1, GW), lambda b, g, c: (0, g)),
            pl.BlockSpec((2 * V7X_LANES, GW), lambda b, g, c: (0, 0)),
        ],
        out_specs=pl.BlockSpec((L, GW), lambda b, g, c: (cur(b, g, c), g)),
        out_shape=jax.ShapeDtypeStruct((bsz * t_len, cfg.d_inner), BF16),
        scratch_shapes=[pltpu.VMEM((N, GW), F32)],
        compiler_params=_params(("parallel", "parallel", "arbitrary"), vmem),
        name="ssd_scan",
    )(xbc, xbc, xbc, xbc, xbc, xbc, z, dt_pad,
      conv_w, conv_w, conv_w, conv_b.reshape(1, -1), conv_b.reshape(1, -1), conv_b.reshape(1, -1),
      pad_heads(dt_bias), pad_heads(a_log), dskip_e, norm_w.reshape(1, -1), expand)


def _mlstm_kernel(ib_ref, fb_ref, q_ref, k_ref, v_ref, o_ref, gc_ref, gr_ref, nw_ref,
                  out_ref, c_ref, n_ref, m_ref, *, chunk, n_heads, dk):
    h = pl.program_id(1)
    c = pl.program_id(2)
    L = chunk

    @pl.when(c == 0)
    def _():
        c_ref[...] = jnp.zeros_like(c_ref)
        n_ref[...] = jnp.zeros_like(n_ref)
        m_ref[...] = jnp.zeros_like(m_ref)

    ib = ib_ref[h]
    fb = fb_ref[h]
    gc = pltpu.roll(gc_ref[...], (V7X_LANES - h) % V7X_LANES, 1)
    li_c = gc[:, 0:1] + ib
    lf_c = _log_sigmoid(gc[:, n_heads:n_heads + 1] + fb)
    gr = gr_ref[...]
    li_r = gr_ref[pl.ds(h, 1), :] + ib
    lf_r = _log_sigmoid(gr_ref[pl.ds(n_heads + h, 1), :] + fb)
    del gr

    r2 = lax.broadcasted_iota(jnp.int32, (L, L), 0)
    c2 = lax.broadcasted_iota(jnp.int32, (L, L), 1)
    causal = c2 <= r2
    tri = jnp.where(causal, 1.0, 0.0).astype(F32)
    tri_t = jnp.where(r2 <= c2, 1.0, 0.0).astype(F32)
    bcum_c = _cumsum_rows(tri, jnp.broadcast_to(lf_c, (L, V7X_LANES)))[:, 0:1]
    bcum_r = jnp.dot(jnp.broadcast_to(lf_r, (V7X_SUBLANES, L)), tri_t, preferred_element_type=F32,
                     precision=lax.Precision.HIGHEST)[0:1, :]
    b_last = bcum_c[L - 1:L, :]

    m_prev = m_ref[...]
    dlog = jnp.where(causal, bcum_c - bcum_r + li_r, -jnp.inf)
    inter_log = bcum_c + m_prev
    m_t = jnp.maximum(inter_log, jnp.max(dlog, axis=1, keepdims=True))
    w_intra = jnp.exp(dlog - m_t)
    w_inter = jnp.exp(inter_log - m_t)

    q = q_ref[...]
    k = k_ref[...]
    v = v_ref[...]
    scale = dk ** -0.5
    s = lax.dot_general(q, k, (((1,), (1,)), ((), ())), preferred_element_type=F32) * scale * w_intra
    c_st = c_ref[...]
    n_st = n_ref[...]
    qf = q.astype(F32) * scale
    num = jnp.dot(s.astype(BF16), v, preferred_element_type=F32)
    num = num + w_inter * (jnp.dot(q, c_st.astype(BF16), preferred_element_type=F32) * scale)
    den = jnp.sum(s, axis=1, keepdims=True) + w_inter * jnp.sum(qf * n_st, axis=1, keepdims=True)
    hid = num / jnp.maximum(jnp.abs(den), jnp.exp(-m_t))
    hid = hid * lax.rsqrt(jnp.mean(hid * hid, axis=-1, keepdims=True) + RMS_EPS)
    out_ref[...] = (hid * nw_ref[...] * jax.nn.sigmoid(o_ref[...].astype(F32))).astype(out_ref.dtype)

    end_log = b_last - bcum_c + li_c
    m_new = jnp.maximum(b_last + m_prev, jnp.max(end_log, axis=0, keepdims=True))
    w_end = jnp.exp(end_log - m_new)
    w_carry = jnp.exp(b_last + m_prev - m_new)
    kw = k.astype(F32) * w_end
    c_ref[...] = w_carry * c_st + jnp.dot(kw.T.astype(BF16), v, preferred_element_type=F32)
    n_ref[...] = w_carry * n_st + jnp.sum(kw, axis=0, keepdims=True)
    m_ref[...] = m_new


def _mlstm_branch(cfg, qk, v, o_pre, gate_cols, gate_rows, igate_b, fgate_b, norm_w, bsz, t_len):
    L = cfg.ml_chunk
    H = cfg.ml_heads
    dk, dv = cfg.ml_dk, cfg.ml_dv
    nc = t_len // L
    assert dk % V7X_LANES == 0 and dv % V7X_LANES == 0
    row = lambda b, h, c: b * nc + c
    kernel = functools.partial(_mlstm_kernel, chunk=L, n_heads=H, dk=dk)
    vmem = 4 * (2 * _nbytes((L, dk), BF16) + 3 * _nbytes((L, dv), BF16)) + 3 * _nbytes((dk, dv), F32) \
        + 10 * _nbytes((L, L), F32) + 8 * _nbytes((L, dv), F32)
    smem = pl.BlockSpec(memory_space=pltpu.SMEM)
    return pl.pallas_call(
        kernel,
        grid=(bsz, H, nc),
        in_specs=[
            smem, smem,
            pl.BlockSpec((L, dk), lambda b, h, c: (row(b, h, c), h)),
            pl.BlockSpec((L, dk), lambda b, h, c: (row(b, h, c), H + h)),
            pl.BlockSpec((L, dv), lambda b, h, c: (row(b, h, c), h)),
            pl.BlockSpec((L, dv), lambda b, h, c: (row(b, h, c), h)),
            pl.BlockSpec((L, V7X_LANES), lambda b, h, c: (row(b, h, c), 0)),
            pl.BlockSpec((2 * H, L), lambda b, h, c: (0, row(b, h, c))),
            pl.BlockSpec((1, dv), lambda b, h, c: (0, h)),
        ],
        out_specs=pl.BlockSpec((L, dv), lambda b, h, c: (row(b, h, c), h)),
        out_shape=jax.ShapeDtypeStruct((bsz * t_len, cfg.ml_v), BF16),
        scratch_shapes=[pltpu.VMEM((dk, dv), F32), pltpu.VMEM((1, dk), F32), pltpu.VMEM((1, 1), F32)],
        compiler_params=_params(("parallel", "parallel", "arbitrary"), vmem),
        name="mlstm_scan",
    )(igate_b, fgate_b, qk, qk, v, o_pre, gate_cols, gate_rows, norm_w.reshape(1, -1))


def _peer_scores_kernel(q_ref, k1_ref, k2_ref, s_ref, *, half):
    q = q_ref[...]
    nt = (((1,), (1,)), ((), ()))
    s_ref[0] = lax.dot_general(k1_ref[...], q[:, :half], nt, preferred_element_type=F32)
    s_ref[1] = lax.dot_general(k2_ref[...], q[:, half:], nt, preferred_element_type=F32)


def _peer_scores(cfg, q, k1, k2):
    n = q.shape[0]
    H, nk, half = cfg.peer_heads, cfg.peer_nkeys, cfg.peer_half
    tm = _pick(n, (1024, 512, 256, 128))
    vmem = 4 * _nbytes((tm, 2 * half), BF16) + 4 * _nbytes((2, nk, tm), F32)
    return pl.pallas_call(
        functools.partial(_peer_scores_kernel, half=half),
        grid=(n // tm, H),
        in_specs=[
            pl.BlockSpec((tm, 2 * half), lambda i, h: (i, h)),
            pl.BlockSpec((None, nk, half), lambda i, h: (h, 0, 0)),
            pl.BlockSpec((None, nk, half), lambda i, h: (h, 0, 0)),
        ],
        out_specs=pl.BlockSpec((None, 2, nk, tm), lambda i, h: (h, 0, 0, i)),
        out_shape=jax.ShapeDtypeStruct((H, 2, nk, n), F32),
        compiler_params=_params(("parallel", "arbitrary"), vmem),
        name="peer_scores",
    )(q, k1, k2)


def _candidate_pairs(topk):
    return [(i, j) for i in range(topk) for j in range(topk) if (i + 1) * (j + 1) <= topk]


def _peer_stats_kernel(s_ref, tau_ref, mu_ref, *, nkeys, topk):
    shape = tau_ref.shape
    neg = jnp.full(shape, -jnp.inf, F32)

    def insert(lst, x):
        out = []
        for cur in lst:
            out.append(jnp.maximum(cur, x))
            x = jnp.minimum(cur, x)
        return tuple(out)

    def top_of_half(r):
        def body(kk, lst):
            return insert(lst, s_ref[r, kk])
        return lax.fori_loop(0, nkeys, body, tuple(neg for _ in range(topk)), unroll=STATS_UNROLL)

    a = top_of_half(0)
    b = top_of_half(1)
    cands = [a[i] + b[j] for (i, j) in _candidate_pairs(topk)]
    best = tuple(neg for _ in range(topk))
    for cnd in cands:
        best = insert(best, cnd)
    tau = best[topk - 1]
    m = best[0]
    z = jnp.zeros(shape, F32)
    for cnd in cands:
        z = z + jnp.where(cnd >= tau, jnp.exp(cnd - m), 0.0)
    tau_ref[...] = tau
    mu_ref[...] = m + jnp.log(z)


def _peer_stats(cfg, s):
    H, _, nk, n = s.shape
    rows = n // V7X_LANES
    tr = _pick(rows, (2 * V7X_SUBLANES, V7X_SUBLANES))
    s5 = s.reshape(H, 2, nk, rows, V7X_LANES)
    vmem = 4 * _nbytes((2, nk, tr, V7X_LANES), F32)
    out = jax.ShapeDtypeStruct((H, rows, V7X_LANES), F32)
    tau, mu = pl.pallas_call(
        functools.partial(_peer_stats_kernel, nkeys=nk, topk=cfg.peer_topk),
        grid=(H, rows // tr),
        in_specs=[pl.BlockSpec((None, 2, nk, tr, V7X_LANES), lambda h, i: (h, 0, 0, i, 0))],
        out_specs=[pl.BlockSpec((None, tr, V7X_LANES), lambda h, i: (h, i, 0))] * 2,
        out_shape=[out, out],
        compiler_params=_params(("parallel", "parallel"), vmem),
        name="peer_stats",
    )(s5)
    return tau.reshape(H, n), mu.reshape(H, n)


def _peer_dense_kernel(xt_ref, u_ref, v_ref, s1_ref, s2_ref, tau_ref, mu_ref, yt_ref, st_scr, w_scr,
                       *, n_heads, sub, nkeys, pieces, d_chunk):
    j = pl.program_id(1)

    @pl.when(j == 0)
    def _():
        yt_ref[...] = jnp.zeros_like(yt_ref)

    rows_p = sub * nkeys // pieces
    sub_p = sub // pieces
    tn = (((0,), (0,)), ((), ()))
    xt = xt_ref[...]
    for p in range(pieces):
        st_scr[p] = jnp.dot(u_ref[p * rows_p:(p + 1) * rows_p, :], xt, preferred_element_type=F32)
    for p in range(pieces):
        for a in range(sub_p):
            i1 = j * sub + p * sub_p + a
            w = None
            for h in range(n_heads):
                cnd = s1_ref[h, pl.ds(i1, 1), :] + s2_ref[h]
                wh = jnp.where(cnd >= tau_ref[pl.ds(h, 1), :], jnp.exp(cnd - mu_ref[pl.ds(h, 1), :]), 0.0)
                w = wh if w is None else w + wh
            st = st_scr[p, a * nkeys:(a + 1) * nkeys, :]
            act = 0.5 * st * (1.0 + lax.erf(st * (2.0 ** -0.5)))
            w_scr[p, a * nkeys:(a + 1) * nkeys, :] = (w * act).astype(BF16)
        for dc in range(yt_ref.shape[0] // d_chunk):
            rows = pl.ds(dc * d_chunk, d_chunk)
            yt_ref[rows, :] += lax.dot_general(v_ref[p * rows_p:(p + 1) * rows_p, rows], w_scr[p], tn,
                                               preferred_element_type=F32)


def _peer_dense(cfg, xt_bf16, u_bf16, vt_bf16, s, tau, mu):
    d, n = xt_bf16.shape
    H, nk = cfg.peer_heads, cfg.peer_nkeys
    E = cfg.peer_experts
    tm = _pick(n, (512, 256, 128))
    te = _pick(E, (512, 256, 128))
    d_chunk = _pick(d, (512, 256, 128))
    sub = te // nk
    pieces = 2 if sub % 2 == 0 else 1
    assert nk == V7X_LANES and te % nk == 0
    once = pl.Buffered(1)
    vmem = (_nbytes((tm, d), BF16) + 4 * _nbytes((te, d), BF16) + 2 * _nbytes((H, nk, tm), F32)
            + 2 * _nbytes((d, tm), F32) + 12 * _nbytes((te, tm), F32) + 2 * _nbytes((d_chunk, tm), F32))
    return pl.pallas_call(
        functools.partial(_peer_dense_kernel, n_heads=H, sub=sub, nkeys=nk, pieces=pieces, d_chunk=d_chunk),
        grid=(n // tm, E // te),
        in_specs=[
            pl.BlockSpec((d, tm), lambda i, j: (0, i), pipeline_mode=once),
            pl.BlockSpec((te, d), lambda i, j: (j, 0)),
            pl.BlockSpec((te, d), lambda i, j: (j, 0)),
            pl.BlockSpec((H, None, nk, tm), lambda i, j: (0, 0, 0, i), pipeline_mode=once),
            pl.BlockSpec((H, None, nk, tm), lambda i, j: (0, 1, 0, i), pipeline_mode=once),
            pl.BlockSpec((H, tm), lambda i, j: (0, i), pipeline_mode=once),
            pl.BlockSpec((H, tm), lambda i, j: (0, i), pipeline_mode=once),
        ],
        out_specs=pl.BlockSpec((d, tm), lambda i, j: (0, i)),
        out_shape=jax.ShapeDtypeStruct((d, n), F32),
        scratch_shapes=[pltpu.VMEM((pieces, te // pieces, tm), F32), pltpu.VMEM((pieces, te // pieces, tm), BF16)],
        compiler_params=_params(("parallel", "arbitrary"), vmem),
        name="peer_dense",
    )(xt_bf16, u_bf16, vt_bf16, s, s, tau, mu)


def _mixer(cfg, h_bf16, bsz, t_len, l, w_in_stack, conv_w, conv_b, dt_bias, a_log, d_skip, ssm_norm_w, w_branch_a_stack,
           igate_b, fgate_b, mlstm_norm_w, w_branch_b_stack, gate_b, w_out_stack):
    d = cfg.d_model
    H = cfg.ml_heads
    splits = (cfg.d_inner, cfg.conv_dim, cfg.ssd_heads, cfg.ml_qk, cfg.ml_qk, cfg.ml_v, cfg.ml_v, H, H, d, d)
    offs = [0]
    for sp in splits:
        offs.append(offs[-1] + sp)
    total = offs[-1]
    bounds = [(offs[0], offs[1]), (offs[1], offs[2]), (offs[2], offs[3]), (offs[3], offs[5]), (offs[5], offs[6]),
              (offs[6], offs[7]), (offs[7], offs[9]), (offs[9], offs[11])]
    padded = [(a, -(-(b - a) // V7X_LANES) * V7X_LANES, b - a) for a, b in bounds]
    in_kernel = [k for k, (a, w, _) in enumerate(padded) if a + w <= total]
    prepared = dict(zip(in_kernel, _split_cast_layer(w_in_stack, l, [padded[k] for k in in_kernel], "prep_w_in")))
    for k, (a, w, valid) in enumerate(padded):
        if k not in prepared:
            prepared[k] = jnp.pad(w_in_stack[l, :, a:a + valid], ((0, 0), (0, w - valid))).astype(BF16)
    w_z, w_xbc, w_dt, w_qk, w_v, w_o, w_gc, w_g = (prepared[k] for k in range(len(padded)))
    w_gr = w_in_stack[l, :, offs[7]:offs[9]].T.astype(BF16)

    z = _matmul(h_bf16, w_z, BF16, "proj_z")
    xbc = _matmul(h_bf16, w_xbc, BF16, "proj_xbc")
    dt_pad = _matmul(h_bf16, w_dt, F32, "proj_dt")
    qk = _matmul(h_bf16, w_qk, BF16, "proj_qk")
    v = _matmul(h_bf16, w_v, BF16, "proj_v")
    o_pre = _matmul(h_bf16, w_o, BF16, "proj_o")
    gate_cols = _matmul(h_bf16, w_gc, F32, "proj_gate_cols")
    gate_rows = _matmul_nt(w_gr, h_bf16, F32, "proj_gate_rows")
    gates = _matmul(h_bf16, w_g, BF16, "proj_gates")

    y_a = _ssd_branch(cfg, z, xbc, dt_pad, conv_w, conv_b, dt_bias, a_log, d_skip, ssm_norm_w, bsz, t_len)
    y_b = _mlstm_branch(cfg, qk, v, o_pre, gate_cols, gate_rows, igate_b, fgate_b, mlstm_norm_w, bsz, t_len)
    mixed = _branch_mix(y_a, y_b, _cast_layer(w_branch_a_stack, l, "prep_w_branch_a"),
                        _cast_layer(w_branch_b_stack, l, "prep_w_branch_b"), gates, gate_b, d)
    return _matmul(mixed, _cast_layer(w_out_stack, l, "prep_w_out"), F32, "proj_out")


def _peer(cfg, h_bf16, ht_bf16, l, wq_stack, k1, k2, u_stack, v_stack):
    q = _matmul(h_bf16, _cast_layer(wq_stack, l, "prep_peer_wq"), BF16, "peer_q")
    s = _peer_scores(cfg, q, k1.astype(BF16), k2.astype(BF16))
    tau, mu = _peer_stats(cfg, s)
    return _peer_dense(cfg, ht_bf16, _cast_layer(u_stack, l, "prep_peer_u"), _cast_layer(v_stack, l, "prep_peer_v"),
                       s, tau, mu)


def _forward(cfg, x, emb_ln_g, emb_ln_b, w_in, conv_w, conv_b, dt_bias, a_log, d_skip, ssm_norm_w,
             w_branch_a, igate_b, fgate_b, mlstm_norm_w, w_branch_b, gate_b, w_out, ln1_g, ln1_b,
             peer_wq, peer_k1, peer_k2, peer_u, peer_v, ln2_g, ln2_b):
    bsz, t_len, d = x.shape
    n = bsz * t_len
    alpha = (2 * cfg.depth) ** 0.25
    tm = _pick(n, (256, 128))
    h, hb = _ln_call(_ln_kernel, [x.reshape(n, d)], emb_ln_g, emb_ln_b, n, d, tm)
    for l in range(cfg.depth):
        mix = _mixer(cfg, hb, bsz, t_len, l, w_in, conv_w[l], conv_b[l], dt_bias[l], a_log[l], d_skip[l],
                     ssm_norm_w[l], w_branch_a, igate_b[l], fgate_b[l], mlstm_norm_w[l], w_branch_b,
                     gate_b[l], w_out)
        h, hb, hbt = _ln_call(functools.partial(_res_ln_kernel, alpha=alpha), [h, mix], ln1_g[l], ln1_b[l], n, d, tm,
                              emit_transposed=True)
        ffn_t = _peer(cfg, hb, hbt, l, peer_wq, peer_k1[l], peer_k2[l], peer_u, peer_v)
        h, hb = _ln_call(functools.partial(_res_ln_t_kernel, alpha=alpha), [h, ffn_t], ln2_g[l], ln2_b[l], n, d, tm,
                         transposed_last=True)
    return h.reshape(bsz, t_len, d)


def kernel(x, emb_ln_g, emb_ln_b, w_in, conv_w, conv_b, dt_bias, a_log, d_skip, ssm_norm_w, w_branch_a, igate_b,
           fgate_b, mlstm_norm_w, w_branch_b, gate_b, w_out, ln1_g, ln1_b, peer_wq, peer_k1, peer_k2, peer_u,
           peer_v, ln2_g, ln2_b):
    return _forward(Cfg(), x, emb_ln_g, emb_ln_b, w_in, conv_w, conv_b, dt_bias, a_log, d_skip, ssm_norm_w,
                    w_branch_a, igate_b, fgate_b, mlstm_norm_w, w_branch_b, gate_b, w_out, ln1_g, ln1_b,
                    peer_wq, peer_k1, peer_k2, peer_u, peer_v, ln2_g, ln2_b)
```

```python
import functools
import math
from typing import NamedTuple

import jax
import jax.numpy as jnp
from jax import lax
from jax.experimental import pallas as pl
from jax.experimental.pallas import tpu as pltpu

F32 = jnp.float32
BF16 = jnp.bfloat16

V7X_LANES = 128
V7X_SUBLANES = 8
V7X_VMEM_BYTES = 64 * 1024 * 1024
VMEM_BUDGET_BYTES = 56 * 1024 * 1024

STATS_UNROLL = 4

LN_EPS = 1e-5
RMS_EPS = 1e-6


class Cfg(NamedTuple):
    depth: int = 2
    d_model: int = 4096
    d_inner: int = 8192
    ssd_headdim: int = 64
    ssd_groups: int = 8
    ssd_state: int = 128
    ssd_conv: int = 4
    ssd_chunk: int = 128
    ml_heads: int = 8
    ml_qk: int = 2048
    ml_v: int = 4096
    ml_chunk: int = 256
    peer_heads: int = 8
    peer_nkeys: int = 128
    peer_dkey: int = 256
    peer_topk: int = 16

    @property
    def ssd_heads(self):
        return self.d_inner // self.ssd_headdim

    @property
    def group_heads(self):
        return self.ssd_heads // self.ssd_groups

    @property
    def group_width(self):
        return self.d_inner // self.ssd_groups

    @property
    def conv_dim(self):
        return self.d_inner + 2 * self.ssd_groups * self.ssd_state

    @property
    def ml_dk(self):
        return self.ml_qk // self.ml_heads

    @property
    def ml_dv(self):
        return self.ml_v // self.ml_heads

    @property
    def peer_half(self):
        return self.peer_dkey // 2

    @property
    def peer_experts(self):
        return self.peer_nkeys * self.peer_nkeys


def _params(semantics, vmem_bytes, flags=None):
    return pltpu.CompilerParams(dimension_semantics=semantics, flags=flags,
                                vmem_limit_bytes=int(min(max(vmem_bytes, 16 * 1024 * 1024), VMEM_BUDGET_BYTES)))


def _nbytes(shape, dtype):
    return math.prod(shape) * jnp.dtype(dtype).itemsize


def _pick(n, prefs):
    for p in prefs:
        if n % p == 0:
            return p
    return n


def _ln_rows(x, g, b):
    mu = jnp.mean(x, axis=-1, keepdims=True)
    xc = x - mu
    var = jnp.mean(xc * xc, axis=-1, keepdims=True)
    return xc * lax.rsqrt(var + LN_EPS) * g + b


def _ln_kernel(x_ref, g_ref, b_ref, of_ref, ob_ref):
    y = _ln_rows(x_ref[...], g_ref[...], b_ref[...])
    of_ref[...] = y
    ob_ref[...] = y.astype(BF16)


def _res_ln_kernel(h_ref, f_ref, g_ref, b_ref, of_ref, ob_ref, obt_ref, *, alpha):
    y = _ln_rows(alpha * h_ref[...] + f_ref[...], g_ref[...], b_ref[...])
    of_ref[...] = y
    ob_ref[...] = y.astype(BF16)
    obt_ref[...] = y.T.astype(BF16)


def _res_ln_t_kernel(h_ref, ft_ref, g_ref, b_ref, of_ref, ob_ref, *, alpha):
    y = _ln_rows(alpha * h_ref[...] + ft_ref[...].T, g_ref[...], b_ref[...])
    of_ref[...] = y
    ob_ref[...] = y.astype(BF16)


def _ln_call(kernel_fn, row_inputs, g, b, n, d, tm, transposed_last=False, emit_transposed=False):
    row_spec = pl.BlockSpec((tm, d), lambda i: (i, 0))
    col_spec = pl.BlockSpec((d, tm), lambda i: (0, i))
    vec_spec = pl.BlockSpec((1, d), lambda i: (0, 0))
    in_specs = [col_spec if (transposed_last and k == len(row_inputs) - 1) else row_spec
                for k in range(len(row_inputs))]
    out_specs = [row_spec, row_spec]
    out_shape = [jax.ShapeDtypeStruct((n, d), F32), jax.ShapeDtypeStruct((n, d), BF16)]
    if emit_transposed:
        out_specs.append(col_spec)
        out_shape.append(jax.ShapeDtypeStruct((d, n), BF16))
    vmem = 2 * (len(row_inputs) + 1) * _nbytes((tm, d), F32) + 4 * _nbytes((tm, d), BF16) + 5 * _nbytes((tm, d), F32)
    return pl.pallas_call(
        kernel_fn,
        grid=(n // tm,),
        in_specs=in_specs + [vec_spec, vec_spec],
        out_specs=out_specs,
        out_shape=out_shape,
        compiler_params=_params(("parallel",), vmem),
    )(*row_inputs, g.reshape(1, d), b.reshape(1, d))


def _mm_kernel(a_ref, b_ref, o_ref):
    o_ref[...] = jnp.dot(a_ref[...], b_ref[...], preferred_element_type=F32).astype(o_ref.dtype)


def _matmul(a, b, out_dtype, name):
    m, k = a.shape
    _, n = b.shape
    tn = _pick(n, (512, 256, 128))
    tm = _pick(m, (1024, 512, 256, 128)) if k <= 4096 else _pick(m, (512, 256, 128))
    vmem = 2 * (_nbytes((tm, k), BF16) + _nbytes((k, tn), BF16) + _nbytes((tm, tn), out_dtype)) + 2 * _nbytes((tm, tn), F32)
    return pl.pallas_call(
        _mm_kernel,
        grid=(m // tm, n // tn),
        in_specs=[pl.BlockSpec((tm, k), lambda i, j: (i, 0)), pl.BlockSpec((k, tn), lambda i, j: (0, j))],
        out_specs=pl.BlockSpec((tm, tn), lambda i, j: (i, j)),
        out_shape=jax.ShapeDtypeStruct((m, n), out_dtype),
        compiler_params=_params(("parallel", "arbitrary"), vmem),
        name=name,
    )(a, b)


def _mm_nt_kernel(w_ref, a_ref, o_ref):
    o_ref[...] = lax.dot_general(w_ref[...], a_ref[...], (((1,), (1,)), ((), ())),
                                 preferred_element_type=F32).astype(o_ref.dtype)


def _matmul_nt(w_t, a, out_dtype, name):
    f, k = w_t.shape
    m, _ = a.shape
    tm = _pick(m, (1024, 512, 256, 128))
    vmem = 2 * (_nbytes((tm, k), BF16) + _nbytes((f, k), BF16) + _nbytes((f, tm), out_dtype)) + 2 * _nbytes((f, tm), F32)
    return pl.pallas_call(
        _mm_nt_kernel,
        grid=(m // tm,),
        in_specs=[pl.BlockSpec((f, k), lambda i: (0, 0)), pl.BlockSpec((tm, k), lambda i: (i, 0))],
        out_specs=pl.BlockSpec((f, tm), lambda i: (0, i)),
        out_shape=jax.ShapeDtypeStruct((f, m), out_dtype),
        compiler_params=_params(("parallel",), vmem),
        name=name,
    )(w_t, a)


def _branch_mix_kernel(ya_ref, yb_ref, wa_ref, wb_ref, ga_ref, gb_ref, ba_ref, bb_ref, o_ref):
    acc_a = jnp.dot(ya_ref[...], wa_ref[...], preferred_element_type=F32)
    acc_b = jnp.dot(yb_ref[...], wb_ref[...], preferred_element_type=F32)
    gate_a = jax.nn.sigmoid(ga_ref[...].astype(F32) + ba_ref[...])
    gate_b = jax.nn.sigmoid(gb_ref[...].astype(F32) + bb_ref[...])
    o_ref[...] = (gate_a * acc_a + gate_b * acc_b).astype(o_ref.dtype)


def _branch_mix(ya, yb, wa, wb, gates, gate_bias, d):
    n, ka = ya.shape
    _, kb = yb.shape
    tm = _pick(n, (512, 256, 128))
    tn = _pick(d, (256, 128))
    nb = d // tn
    vmem = 2 * (_nbytes((tm, ka + kb), BF16) + _nbytes((ka + kb, tn), BF16) + 3 * _nbytes((tm, tn), BF16)) + 6 * _nbytes((tm, tn), F32)
    bias2 = gate_bias.reshape(1, 2 * d)
    return pl.pallas_call(
        _branch_mix_kernel,
        grid=(n // tm, nb),
        in_specs=[
            pl.BlockSpec((tm, ka), lambda i, j: (i, 0)),
            pl.BlockSpec((tm, kb), lambda i, j: (i, 0)),
            pl.BlockSpec((ka, tn), lambda i, j: (0, j)),
            pl.BlockSpec((kb, tn), lambda i, j: (0, j)),
            pl.BlockSpec((tm, tn), lambda i, j: (i, j)),
            pl.BlockSpec((tm, tn), lambda i, j: (i, j + nb)),
            pl.BlockSpec((1, tn), lambda i, j: (0, j)),
            pl.BlockSpec((1, tn), lambda i, j: (0, j + nb)),
        ],
        out_specs=pl.BlockSpec((tm, tn), lambda i, j: (i, j)),
        out_shape=jax.ShapeDtypeStruct((n, d), BF16),
        compiler_params=_params(("parallel", "arbitrary"), vmem),
        name="branch_mix",
    )(ya, yb, wa, wb, gates, gates, bias2, bias2)


CAST_BLOCK_BYTES = 8 * 1024 * 1024


def _cast_rows(rows, cols):
    for tr in (1024, 512, 256, 128, 64, 32, 16, 8):
        if rows % tr == 0 and tr * cols * 4 <= CAST_BLOCK_BYTES:
            return tr
    return rows


def _cast_kernel(w_ref, o_ref):
    o_ref[...] = w_ref[...].astype(BF16)


def _cast_layer(w_stack, l, name):
    _, r, c = w_stack.shape
    tr = _cast_rows(r, c)
    vmem = 3 * _nbytes((tr, c), F32) + 2 * _nbytes((tr, c), BF16)
    return pl.pallas_call(
        _cast_kernel,
        grid=(r // tr,),
        in_specs=[pl.BlockSpec((None, tr, c), lambda i: (l, i, 0))],
        out_specs=pl.BlockSpec((tr, c), lambda i: (i, 0)),
        out_shape=jax.ShapeDtypeStruct((r, c), BF16),
        compiler_params=_params(("parallel",), vmem),
        name=name,
    )(w_stack)


def _split_cast_kernel(w_ref, *o_refs, segs):
    for o_ref, (start, width, valid) in zip(o_refs, segs):
        blk = w_ref[:, start:start + width]
        if valid < width:
            lane = lax.broadcasted_iota(jnp.int32, blk.shape, 1)
            blk = jnp.where(lane < valid, blk, 0.0)
        o_ref[...] = blk.astype(BF16)


def _split_cast_layer(w_stack, l, segs, name):
    _, r, c = w_stack.shape
    tr = _cast_rows(r, c)
    total = sum(w for _, w, _ in segs)
    vmem = 3 * _nbytes((tr, c), F32) + 3 * _nbytes((tr, total), BF16)
    return pl.pallas_call(
        functools.partial(_split_cast_kernel, segs=tuple(segs)),
        grid=(r // tr,),
        in_specs=[pl.BlockSpec((None, tr, c), lambda i: (l, i, 0))],
        out_specs=[pl.BlockSpec((tr, w), lambda i: (i, 0)) for _, w, _ in segs],
        out_shape=[jax.ShapeDtypeStruct((r, w), BF16) for _, w, _ in segs],
        compiler_params=_params(("parallel",), vmem),
        name=name,
    )(w_stack)


def _split_dot(x, m_bf16):
    hi = x.astype(BF16)
    lo = (x - hi.astype(F32)).astype(BF16)
    return jnp.dot(jnp.concatenate([hi, lo], axis=1), m_bf16, preferred_element_type=F32)


def _cumsum_rows(tri, x):
    return jnp.dot(tri, x, preferred_element_type=F32, precision=lax.Precision.HIGHEST)


def _softplus(x):
    return jnp.maximum(x, 0.0) + jnp.log1p(jnp.exp(-jnp.abs(x)))


def _log_sigmoid(x):
    return -_softplus(-x)


def _silu(x):
    return x * jax.nn.sigmoid(x)


def _ssd_kernel(xs_ref, xsp_ref, bm_ref, bmp_ref, cm_ref, cmp_ref, z_ref, dt_ref,
                cwx_ref, cwb_ref, cwc_ref, cbx_ref, cbb_ref, cbc_ref,
                dtb_ref, alog_ref, dskip_ref, nw_ref, expand_ref,
                o_ref, state_ref, *, chunk, headdim, n_taps, group_heads):
    c = pl.program_id(2)
    L = chunk
    first = c == 0

    @pl.when(first)
    def _():
        state_ref[...] = jnp.zeros_like(state_ref)

    row = lax.broadcasted_iota(jnp.int32, (L, 2 * L), 0)
    col = lax.broadcasted_iota(jnp.int32, (L, 2 * L), 1)
    keep_prev = jnp.logical_or(col >= L, jnp.logical_not(first))

    def shift_mat(k):
        hit = jnp.logical_and(col == row + (L - (n_taps - 1 - k)), keep_prev)
        return jnp.where(hit, 1.0, 0.0).astype(BF16)

    shifts = [shift_mat(k) for k in range(n_taps - 1)]

    def conv(cur_ref, prev_ref, w_ref, b_ref):
        cur = cur_ref[...]
        stack = jnp.concatenate([prev_ref[...], cur], axis=0)
        w = w_ref[...]
        acc = cur.astype(F32) * w[n_taps - 1:n_taps, :] + b_ref[...]
        for k in range(n_taps - 1):
            acc = acc + jnp.dot(shifts[k], stack, preferred_element_type=F32) * w[k:k + 1, :]
        return _silu(acc)

    xs = conv(xs_ref, xsp_ref, cwx_ref, cbx_ref)
    bmat = conv(bm_ref, bmp_ref, cwb_ref, cbb_ref)
    cmat = conv(cm_ref, cmp_ref, cwc_ref, cbc_ref)

    lane0 = (pl.program_id(1) * group_heads) % V7X_LANES
    dt_pre = pltpu.roll(dt_ref[...], (V7X_LANES - lane0) % V7X_LANES, 1)
    dt = _softplus(dt_pre + dtb_ref[...])
    la = dt * (-jnp.exp(alog_ref[...]))
    r2 = lax.broadcasted_iota(jnp.int32, (L, L), 0)
    c2 = lax.broadcasted_iota(jnp.int32, (L, L), 1)
    causal = c2 <= r2
    tri = jnp.where(causal, 1.0, 0.0).astype(F32)
    acum = _cumsum_rows(tri, la)
    acum_t = acum.T
    a_last = acum[L - 1:L, :]

    expand = expand_ref[...]
    dt_e = _split_dot(dt, expand)
    eac_e = _split_dot(jnp.exp(acum), expand)
    wend_e = _split_dot(jnp.exp(a_last - acum), expand)

    xdt = xs * dt_e
    xdt_b = xdt.astype(BF16)
    cb = cmat.astype(BF16)
    bb = bmat.astype(BF16)
    scores = lax.dot_general(cb, bb, (((1,), (1,)), ((), ())), preferred_element_type=F32)

    gw = xs.shape[1]
    heads_per_tile = V7X_LANES // headdim
    lane = lax.broadcasted_iota(jnp.int32, (L, V7X_LANES), 1)
    y_tiles = []
    for q in range(gw // V7X_LANES):
        ms = []
        rhs = []
        x_tile = xdt_b[:, q * V7X_LANES:(q + 1) * V7X_LANES]
        for e in range(heads_per_tile):
            j = q * heads_per_tile + e
            seg = acum[:, j:j + 1] - acum_t[j:j + 1, :]
            ms.append((scores * jnp.where(causal, jnp.exp(seg), 0.0)).astype(BF16))
            in_head = jnp.logical_and(lane >= e * headdim, lane < (e + 1) * headdim)
            rhs.append(jnp.where(in_head, x_tile, jnp.zeros_like(x_tile)))
        y_tiles.append(jnp.dot(jnp.concatenate(ms, axis=1), jnp.concatenate(rhs, axis=0),
                               preferred_element_type=F32))
    y = jnp.concatenate(y_tiles, axis=1)

    state = state_ref[...]
    y = y + jnp.dot(cb, state.astype(BF16), preferred_element_type=F32) * eac_e
    y = y + xs * dskip_ref[...]
    y = y * _silu(z_ref[...].astype(F32))
    y = y * lax.rsqrt(jnp.mean(y * y, axis=-1, keepdims=True) + RMS_EPS)
    o_ref[...] = (y * nw_ref[...]).astype(o_ref.dtype)

    upd = jnp.dot(bmat.T.astype(BF16), (wend_e * xdt).astype(BF16), preferred_element_type=F32)
    state_ref[...] = state * eac_e[L - 1:L, :] + upd


def _ssd_branch(cfg, z, xbc, dt_pad, conv_w, conv_b, dt_bias, a_log, d_skip, norm_w, bsz, t_len):
    L = cfg.ssd_chunk
    G = cfg.ssd_groups
    GW = cfg.group_width
    N = cfg.ssd_state
    J = cfg.group_heads
    nc = t_len // L
    assert V7X_LANES % cfg.ssd_headdim == 0 and J <= V7X_LANES and GW % V7X_LANES == 0 and N % V7X_LANES == 0
    nx = cfg.d_inner // GW
    b_off = cfg.d_inner // N
    c_off = (cfg.d_inner + G * N) // N

    def pad_heads(v):
        return jnp.pad(v.reshape(G, 1, J), ((0, 0), (0, 0), (0, V7X_LANES - J)))

    dskip_e = jnp.repeat(d_skip, cfg.ssd_headdim).reshape(1, cfg.d_inner)
    head_of_chan = jnp.arange(GW) // cfg.ssd_headdim
    expand = (jnp.arange(V7X_LANES)[:, None] == head_of_chan[None, :]).astype(BF16)
    expand = jnp.concatenate([expand, expand], axis=0)

    cur = lambda b, g, c: b * nc + c
    prev = lambda b, g, c: b * nc + jnp.maximum(c - 1, 0)
    kernel = functools.partial(_ssd_kernel, chunk=L, headdim=cfg.ssd_headdim, n_taps=cfg.ssd_conv, group_heads=J)
    vmem = 4 * _nbytes((L, GW), BF16) * 2 + 24 * _nbytes((L, GW), F32) + 2 * _nbytes((N, GW), F32)
    return pl.pallas_call(
        kernel,
        grid=(bsz, G, nc),
        in_specs=[
            pl.BlockSpec((L, GW), lambda b, g, c: (cur(b, g, c), g)),
            pl.BlockSpec((L, GW), lambda b, g, c: (prev(b, g, c), g)),
            pl.BlockSpec((L, N), lambda b, g, c: (cur(b, g, c), b_off + g)),
            pl.BlockSpec((L, N), lambda b, g, c: (prev(b, g, c), b_off + g)),
            pl.BlockSpec((L, N), lambda b, g, c: (cur(b, g, c), c_off + g)),
            pl.BlockSpec((L, N), lambda b, g, c: (prev(b, g, c), c_off + g)),
            pl.BlockSpec((L, GW), lambda b, g, c: (cur(b, g, c), g)),
            pl.BlockSpec((L, V7X_LANES), lambda b, g, c: (cur(b, g, c), (g * J) // V7X_LANES)),
            pl.BlockSpec((cfg.ssd_conv, GW), lambda b, g, c: (0, g)),
            pl.BlockSpec((cfg.ssd_conv, N), lambda b, g, c: (0, b_off + g)),
            pl.BlockSpec((cfg.ssd_conv, N), lambda b, g, c: (0, c_off + g)),
            pl.BlockSpec((1, GW), lambda b, g, c: (0, g)),
            pl.BlockSpec((1, N), lambda b, g, c: (0, b_off + g)),
            pl.BlockSpec((1, N), lambda b, g, c: (0, c_off + g)),
            pl.BlockSpec((None, 1, V7X_LANES), lambda b, g, c: (g, 0, 0)),
            pl.BlockSpec((None, 1, V7X_LANES), lambda b, g, c: (g, 0, 0)),
            pl.BlockSpec((1, GW), lambda b, g, c: (0, g)),
            pl.BlockSpec((1, GW), lambda b, g, c: (0, g)),
            pl.BlockSpec((2 * V7X_LANES, GW), lambda b, g, c: (0, 0)),
        ],
        out_specs=pl.BlockSpec((L, GW), lambda b, g, c: (cur(b, g, c), g)),
        out_shape=jax.ShapeDtypeStruct((bsz * t_len, cfg.d_inner), BF16),
        scratch_shapes=[pltpu.VMEM((N, GW), F32)],
        compiler_params=_params(("parallel", "parallel", "arbitrary"), vmem),
        name="ssd_scan",
    )(xbc, xbc, xbc, xbc, xbc, xbc, z, dt_pad,
      conv_w, conv_w, conv_w, conv_b.reshape(1, -1), conv_b.reshape(1, -1), conv_b.reshape(1, -1),
      pad_heads(dt_bias), pad_heads(a_log), dskip_e, norm_w.reshape(1, -1), expand)


def _mlstm_kernel(ib_ref, fb_ref, q_ref, k_ref, v_ref, o_ref, gc_ref, gr_ref, nw_ref,
                  out_ref, c_ref, n_ref, m_ref, *, chunk, n_heads, dk):
    h = pl.program_id(1)
    c = pl.program_id(2)
    L = chunk

    @pl.when(c == 0)
    def _():
        c_ref[...] = jnp.zeros_like(c_ref)
        n_ref[...] = jnp.zeros_like(n_ref)
        m_ref[...] = jnp.zeros_like(m_ref)

    ib = ib_ref[h]
    fb = fb_ref[h]
    gc = pltpu.roll(gc_ref[...], (V7X_LANES - h) % V7X_LANES, 1)
    li_c = gc[:, 0:1] + ib
    lf_c = _log_sigmoid(gc[:, n_heads:n_heads + 1] + fb)
    gr = gr_ref[...]
    li_r = gr_ref[pl.ds(h, 1), :] + ib
    lf_r = _log_sigmoid(gr_ref[pl.ds(n_heads + h, 1), :] + fb)
    del gr

    r2 = lax.broadcasted_iota(jnp.int32, (L, L), 0)
    c2 = lax.broadcasted_iota(jnp.int32, (L, L), 1)
    causal = c2 <= r2
    tri = jnp.where(causal, 1.0, 0.0).astype(F32)
    tri_t = jnp.where(r2 <= c2, 1.0, 0.0).astype(F32)
    bcum_c = _cumsum_rows(tri, jnp.broadcast_to(lf_c, (L, V7X_LANES)))[:, 0:1]
    bcum_r = jnp.dot(jnp.broadcast_to(lf_r, (V7X_SUBLANES, L)), tri_t, preferred_element_type=F32,
                     precision=lax.Precision.HIGHEST)[0:1, :]
    b_last = bcum_c[L - 1:L, :]

    m_prev = m_ref[...]
    dlog = jnp.where(causal, bcum_c - bcum_r + li_r, -jnp.inf)
    inter_log = bcum_c + m_prev
    m_t = jnp.maximum(inter_log, jnp.max(dlog, axis=1, keepdims=True))
    w_intra = jnp.exp(dlog - m_t)
    w_inter = jnp.exp(inter_log - m_t)

    q = q_ref[...]
    k = k_ref[...]
    v = v_ref[...]
    scale = dk ** -0.5
    s = lax.dot_general(q, k, (((1,), (1,)), ((), ())), preferred_element_type=F32) * scale * w_intra
    c_st = c_ref[...]
    n_st = n_ref[...]
    qf = q.astype(F32) * scale
    num = jnp.dot(s.astype(BF16), v, preferred_element_type=F32)
    num = num + w_inter * (jnp.dot(q, c_st.astype(BF16), preferred_element_type=F32) * scale)
    den = jnp.sum(s, axis=1, keepdims=True) + w_inter * jnp.sum(qf * n_st, axis=1, keepdims=True)
    hid = num / jnp.maximum(jnp.abs(den), jnp.exp(-m_t))
    hid = hid * lax.rsqrt(jnp.mean(hid * hid, axis=-1, keepdims=True) + RMS_EPS)
    out_ref[...] = (hid * nw_ref[...] * jax.nn.sigmoid(o_ref[...].astype(F32))).astype(out_ref.dtype)

    end_log = b_last - bcum_c + li_c
    m_new = jnp.maximum(b_last + m_prev, jnp.max(end_log, axis=0, keepdims=True))
    w_end = jnp.exp(end_log - m_new)
    w_carry = jnp.exp(b_last + m_prev - m_new)
    kw = k.astype(F32) * w_end
    c_ref[...] = w_carry * c_st + jnp.dot(kw.T.astype(BF16), v, preferred_element_type=F32)
    n_ref[...] = w_carry * n_st + jnp.sum(kw, axis=0, keepdims=True)
    m_ref[...] = m_new


def _mlstm_branch(cfg, qk, v, o_pre, gate_cols, gate_rows, igate_b, fgate_b, norm_w, bsz, t_len):
    L = cfg.ml_chunk
    H = cfg.ml_heads
    dk, dv = cfg.ml_dk, cfg.ml_dv
    nc = t_len // L
    assert dk % V7X_LANES == 0 and dv % V7X_LANES == 0
    row = lambda b, h, c: b * nc + c
    kernel = functools.partial(_mlstm_kernel, chunk=L, n_heads=H, dk=dk)
    vmem = 4 * (2 * _nbytes((L, dk), BF16) + 3 * _nbytes((L, dv), BF16)) + 3 * _nbytes((dk, dv), F32) \
        + 10 * _nbytes((L, L), F32) + 8 * _nbytes((L, dv), F32)
    smem = pl.BlockSpec(memory_space=pltpu.SMEM)
    return pl.pallas_call(
        kernel,
        grid=(bsz, H, nc),
        in_specs=[
            smem, smem,
            pl.BlockSpec((L, dk), lambda b, h, c: (row(b, h, c), h)),
            pl.BlockSpec((L, dk), lambda b, h, c: (row(b, h, c), H + h)),
            pl.BlockSpec((L, dv), lambda b, h, c: (row(b, h, c), h)),
            pl.BlockSpec((L, dv), lambda b, h, c: (row(b, h, c), h)),
            pl.BlockSpec((L, V7X_LANES), lambda b, h, c: (row(b, h, c), 0)),
            pl.BlockSpec((2 * H, L), lambda b, h, c: (0, row(b, h, c))),
            pl.BlockSpec((1, dv), lambda b, h, c: (0, h)),
        ],
        out_specs=pl.BlockSpec((L, dv), lambda b, h, c: (row(b, h, c), h)),
        out_shape=jax.ShapeDtypeStruct((bsz * t_len, cfg.ml_v), BF16),
        scratch_shapes=[pltpu.VMEM((dk, dv), F32), pltpu.VMEM((1, dk), F32), pltpu.VMEM((1, 1), F32)],
        compiler_params=_params(("parallel", "parallel", "arbitrary"), vmem),
        name="mlstm_scan",
    )(igate_b, fgate_b, qk, qk, v, o_pre, gate_cols, gate_rows, norm_w.reshape(1, -1))


def _peer_scores_kernel(q_ref, k1_ref, k2_ref, s_ref, *, half):
    q = q_ref[...]
    nt = (((1,), (1,)), ((), ()))
    s_ref[0] = lax.dot_general(k1_ref[...], q[:, :half], nt, preferred_element_type=F32)
    s_ref[1] = lax.dot_general(k2_ref[...], q[:, half:], nt, preferred_element_type=F32)


def _peer_scores(cfg, q, k1, k2):
    n = q.shape[0]
    H, nk, half = cfg.peer_heads, cfg.peer_nkeys, cfg.peer_half
    tm = _pick(n, (1024, 512, 256, 128))
    vmem = 4 * _nbytes((tm, 2 * half), BF16) + 4 * _nbytes((2, nk, tm), F32)
    return pl.pallas_call(
        functools.partial(_peer_scores_kernel, half=half),
        grid=(n // tm, H),
        in_specs=[
            pl.BlockSpec((tm, 2 * half), lambda i, h: (i, h)),
            pl.BlockSpec((None, nk, half), lambda i, h: (h, 0, 0)),
            pl.BlockSpec((None, nk, half), lambda i, h: (h, 0, 0)),
        ],
        out_specs=pl.BlockSpec((None, 2, nk, tm), lambda i, h: (h, 0, 0, i)),
        out_shape=jax.ShapeDtypeStruct((H, 2, nk, n), F32),
        compiler_params=_params(("parallel", "arbitrary"), vmem),
        name="peer_scores",
    )(q, k1, k2)


def _candidate_pairs(topk):
    return [(i, j) for i in range(topk) for j in range(topk) if (i + 1) * (j + 1) <= topk]


def _peer_stats_kernel(s_ref, tau_ref, mu_ref, *, nkeys, topk):
    shape = tau_ref.shape
    neg = jnp.full(shape, -jnp.inf, F32)

    def insert(lst, x):
        out = []
        for cur in lst:
            out.append(jnp.maximum(cur, x))
            x = jnp.minimum(cur, x)
        return tuple(out)

    def top_of_half(r):
        def body(kk, lst):
            return insert(lst, s_ref[r, kk])
        return lax.fori_loop(0, nkeys, body, tuple(neg for _ in range(topk)), unroll=STATS_UNROLL)

    a = top_of_half(0)
    b = top_of_half(1)
    cands = [a[i] + b[j] for (i, j) in _candidate_pairs(topk)]
    best = tuple(neg for _ in range(topk))
    for cnd in cands:
        best = insert(best, cnd)
    tau = best[topk - 1]
    m = best[0]
    z = jnp.zeros(shape, F32)
    for cnd in cands:
        z = z + jnp.where(cnd >= tau, jnp.exp(cnd - m), 0.0)
    tau_ref[...] = tau
    mu_ref[...] = m + jnp.log(z)


def _peer_stats(cfg, s):
    H, _, nk, n = s.shape
    rows = n // V7X_LANES
    tr = _pick(rows, (2 * V7X_SUBLANES, V7X_SUBLANES))
    s5 = s.reshape(H, 2, nk, rows, V7X_LANES)
    vmem = 4 * _nbytes((2, nk, tr, V7X_LANES), F32)
    out = jax.ShapeDtypeStruct((H, rows, V7X_LANES), F32)
    tau, mu = pl.pallas_call(
        functools.partial(_peer_stats_kernel, nkeys=nk, topk=cfg.peer_topk),
        grid=(H, rows // tr),
        in_specs=[pl.BlockSpec((None, 2, nk, tr, V7X_LANES), lambda h, i: (h, 0, 0, i, 0))],
        out_specs=[pl.BlockSpec((None, tr, V7X_LANES), lambda h, i: (h, i, 0))] * 2,
        out_shape=[out, out],
        compiler_params=_params(("parallel", "parallel"), vmem),
        name="peer_stats",
    )(s5)
    return tau.reshape(H, n), mu.reshape(H, n)


def _peer_dense_kernel(xt_ref, u_ref, v_ref, s1_ref, s2_ref, tau_ref, mu_ref, yt_ref, st_scr, w_scr,
                       *, n_heads, sub, nkeys, pieces, d_chunk):
    j = pl.program_id(1)

    @pl.when(j == 0)
    def _():
        yt_ref[...] = jnp.zeros_like(yt_ref)

    rows_p = sub * nkeys // pieces
    sub_p = sub // pieces
    tn = (((0,), (0,)), ((), ()))
    xt = xt_ref[...]
    for p in range(pieces):
        st_scr[p] = jnp.dot(u_ref[p * rows_p:(p + 1) * rows_p, :], xt, preferred_element_type=F32)
    for p in range(pieces):
        for a in range(sub_p):
            i1 = j * sub + p * sub_p + a
            w = None
            for h in range(n_heads):
                cnd = s1_ref[h, pl.ds(i1, 1), :] + s2_ref[h]
                wh = jnp.where(cnd >= tau_ref[pl.ds(h, 1), :], jnp.exp(cnd - mu_ref[pl.ds(h, 1), :]), 0.0)
                w = wh if w is None else w + wh
            st = st_scr[p, a * nkeys:(a + 1) * nkeys, :]
            act = 0.5 * st * (1.0 + lax.erf(st * (2.0 ** -0.5)))
            w_scr[p, a * nkeys:(a + 1) * nkeys, :] = (w * act).astype(BF16)
        for dc in range(yt_ref.shape[0] // d_chunk):
            rows = pl.ds(dc * d_chunk, d_chunk)
            yt_ref[rows, :] += lax.dot_general(v_ref[p * rows_p:(p + 1) * rows_p, rows], w_scr[p], tn,
                                               preferred_element_type=F32)


def _peer_dense(cfg, xt_bf16, u_bf16, vt_bf16, s, tau, mu):
    d, n = xt_bf16.shape
    H, nk = cfg.peer_heads, cfg.peer_nkeys
    E = cfg.peer_experts
    tm = _pick(n, (512, 256, 128))
    te = _pick(E, (512, 256, 128))
    d_chunk = _pick(d, (512, 256, 128))
    sub = te // nk
    pieces = 2 if sub % 2 == 0 else 1
    assert nk == V7X_LANES and te % nk == 0
    once = pl.Buffered(1)
    vmem = (_nbytes((tm, d), BF16) + 4 * _nbytes((te, d), BF16) + 2 * _nbytes((H, nk, tm), F32)
            + 2 * _nbytes((d, tm), F32) + 12 * _nbytes((te, tm), F32) + 2 * _nbytes((d_chunk, tm), F32))
    return pl.pallas_call(
        functools.partial(_peer_dense_kernel, n_heads=H, sub=sub, nkeys=nk, pieces=pieces, d_chunk=d_chunk),
        grid=(n // tm, E // te),
        in_specs=[
            pl.BlockSpec((d, tm), lambda i, j: (0, i), pipeline_mode=once),
            pl.BlockSpec((te, d), lambda i, j: (j, 0)),
            pl.BlockSpec((te, d), lambda i, j: (j, 0)),
            pl.BlockSpec((H, None, nk, tm), lambda i, j: (0, 0, 0, i), pipeline_mode=once),
            pl.BlockSpec((H, None, nk, tm), lambda i, j: (0, 1, 0, i), pipeline_mode=once),
            pl.BlockSpec((H, tm), lambda i, j: (0, i), pipeline_mode=once),
            pl.BlockSpec((H, tm), lambda i, j: (0, i), pipeline_mode=once),
        ],
        out_specs=pl.BlockSpec((d, tm), lambda i, j: (0, i)),
        out_shape=jax.ShapeDtypeStruct((d, n), F32),
        scratch_shapes=[pltpu.VMEM((pieces, te // pieces, tm), F32), pltpu.VMEM((pieces, te // pieces, tm), BF16)],
        compiler_params=_params(("parallel", "arbitrary"), vmem),
        name="peer_dense",
    )(xt_bf16, u_bf16, vt_bf16, s, s, tau, mu)


def _mixer(cfg, h_bf16, bsz, t_len, l, w_in_stack, conv_w, conv_b, dt_bias, a_log, d_skip, ssm_norm_w, w_branch_a_stack,
           igate_b, fgate_b, mlstm_norm_w, w_branch_b_stack, gate_b, w_out_stack):
    d = cfg.d_model
    H = cfg.ml_heads
    splits = (cfg.d_inner, cfg.conv_dim, cfg.ssd_heads, cfg.ml_qk, cfg.ml_qk, cfg.ml_v, cfg.ml_v, H, H, d, d)
    offs = [0]
    for sp in splits:
        offs.append(offs[-1] + sp)
    total = offs[-1]
    bounds = [(offs[0], offs[1]), (offs[1], offs[2]), (offs[2], offs[3]), (offs[3], offs[5]), (offs[5], offs[6]),
              (offs[6], offs[7]), (offs[7], offs[9]), (offs[9], offs[11])]
    padded = [(a, -(-(b - a) // V7X_LANES) * V7X_LANES, b - a) for a, b in bounds]
    del total
    prepared = {}
    for k, (a, w, valid) in enumerate(padded):
        prepared[k] = jnp.pad(w_in_stack[l, :, a:a + valid], ((0, 0), (0, w - valid))).astype(BF16)
    w_z, w_xbc, w_dt, w_qk, w_v, w_o, w_gc, w_g = (prepared[k] for k in range(len(padded)))
    w_gr = w_gc[:, :2 * H].T

    z = _matmul(h_bf16, w_z, BF16, "proj_z")
    xbc = _matmul(h_bf16, w_xbc, BF16, "proj_xbc")
    dt_pad = _matmul(h_bf16, w_dt, F32, "proj_dt")
    qk = _matmul(h_bf16, w_qk, BF16, "proj_qk")
    v = _matmul(h_bf16, w_v, BF16, "proj_v")
    o_pre = _matmul(h_bf16, w_o, BF16, "proj_o")
    gate_cols = _matmul(h_bf16, w_gc, F32, "proj_gate_cols")
    gate_rows = _matmul_nt(w_gr, h_bf16, F32, "proj_gate_rows")
    gates = _matmul(h_bf16, w_g, BF16, "proj_gates")

    y_a = _ssd_branch(cfg, z, xbc, dt_pad, conv_w, conv_b, dt_bias, a_log, d_skip, ssm_norm_w, bsz, t_len)
    y_b = _mlstm_branch(cfg, qk, v, o_pre, gate_cols, gate_rows, igate_b, fgate_b, mlstm_norm_w, bsz, t_len)
    mixed = _branch_mix(y_a, y_b, _cast_layer(w_branch_a_stack, l, "prep_w_branch_a"),
                        _cast_layer(w_branch_b_stack, l, "prep_w_branch_b"), gates, gate_b, d)
    return _matmul(mixed, _cast_layer(w_out_stack, l, "prep_w_out"), F32, "proj_out")


def _peer(cfg, h_bf16, ht_bf16, l, wq_stack, k1, k2, u_stack, v_stack):
    q = _matmul(h_bf16, _cast_layer(wq_stack, l, "prep_peer_wq"), BF16, "peer_q")
    s = _peer_scores(cfg, q, k1.astype(BF16), k2.astype(BF16))
    tau, mu = _peer_stats(cfg, s)
    return _peer_dense(cfg, ht_bf16, _cast_layer(u_stack, l, "prep_peer_u"), _cast_layer(v_stack, l, "prep_peer_v"),
                       s, tau, mu)


def _forward(cfg, x, emb_ln_g, emb_ln_b, w_in, conv_w, conv_b, dt_bias, a_log, d_skip, ssm_norm_w,
             w_branch_a, igate_b, fgate_b, mlstm_norm_w, w_branch_b, gate_b, w_out, ln1_g, ln1_b,
             peer_wq, peer_k1, peer_k2, peer_u, peer_v, ln2_g, ln2_b):
    bsz, t_len, d = x.shape
    n = bsz * t_len
    alpha = (2 * cfg.depth) ** 0.25
    tm = _pick(n, (256, 128))
    h, hb = _ln_call(_ln_kernel, [x.reshape(n, d)], emb_ln_g, emb_ln_b, n, d, tm)
    for l in range(cfg.depth):
        mix = _mixer(cfg, hb, bsz, t_len, l, w_in, conv_w[l], conv_b[l], dt_bias[l], a_log[l], d_skip[l],
                     ssm_norm_w[l], w_branch_a, igate_b[l], fgate_b[l], mlstm_norm_w[l], w_branch_b,
                     gate_b[l], w_out)
        h, hb, hbt = _ln_call(functools.partial(_res_ln_kernel, alpha=alpha), [h, mix], ln1_g[l], ln1_b[l], n, d, tm,
                              emit_transposed=True)
        ffn_t = _peer(cfg, hb, hbt, l, peer_wq, peer_k1[l], peer_k2[l], peer_u, peer_v)
        h, hb = _ln_call(functools.partial(_res_ln_t_kernel, alpha=alpha), [h, ffn_t], ln2_g[l], ln2_b[l], n, d, tm,
                         transposed_last=True)
    return h.reshape(bsz, t_len, d)


def kernel(x, emb_ln_g, emb_ln_b, w_in, conv_w, conv_b, dt_bias, a_log, d_skip, ssm_norm_w, w_branch_a, igate_b,
           fgate_b, mlstm_norm_w, w_branch_b, gate_b, w_out, ln1_g, ln1_b, peer_wq, peer_k1, peer_k2, peer_u,
           peer_v, ln2_g, ln2_b):
    return _forward(Cfg(), x, emb_ln_g, emb_ln_b, w_in, conv_w, conv_b, dt_bias, a_log, d_skip, ssm_norm_w,
                    w_branch_a, igate_b, fgate_b, mlstm_norm_w, w_branch_b, gate_b, w_out, ln1_g, ln1_b,
                    peer_wq, peer_k1, peer_k2, peer_u, peer_v, ln2_g, ln2_b)
```
